```python
import math
import jax, jax.numpy as jnp
from jax import lax
import numpy as np


D_MODEL = 2048
BATCH = 4
SEQ = 4096
DEPTH = 2

CHUNK = 64
EPS = 1e-6
D_FF = ((8 * D_MODEL) // 3 + 255) // 256 * 256

POOL_WINDOWS = (2, 4, 8, 16)
POOL_WIDTH = D_MODEL // 4
POOL_GROUP = POOL_WIDTH // len(POOL_WINDOWS)
SB_HEAD_DIM = 128
SB_HEADS = (D_MODEL - POOL_WIDTH) // SB_HEAD_DIM
SB_WIDTH = SB_HEADS * SB_HEAD_DIM
SB_BLOCK = 128
EVEN_IN = POOL_WIDTH + 3 * SB_WIDTH
EVEN_MIX = POOL_WIDTH + SB_WIDTH

SSM_WIDTH = D_MODEL // 2
SSM_GROUP = 16
SSM_GROUPS = SSM_WIDTH // SSM_GROUP
SSM_STATE = 64
SGU_WIDTH = D_MODEL // 2
SGU_HEADS = 8
SGU_HEAD_DIM = SGU_WIDTH // SGU_HEADS
SGU_LEN = 128
ODD_IN = SSM_WIDTH + 2 * SGU_WIDTH
ODD_MIX = SSM_WIDTH + SGU_WIDTH

N_EVEN = (DEPTH + 1) // 2
N_ODD = DEPTH // 2

kernel_name = 'hybrid_pool_stickbreak_s5_sgu_macaron'


def rmsnorm(x, g):
    xf = x.astype(jnp.float32)
    y = xf * lax.rsqrt(jnp.mean(xf * xf, axis=-1, keepdims=True) + EPS)
    return (y * g.astype(jnp.float32)).astype(x.dtype)


def swiglu(x, w_gate, w_up, w_down):
    return (jax.nn.silu(x @ w_gate) * (x @ w_up)) @ w_down


def pool_mixer(u, w_group, scale):
    Bsz, L, _ = u.shape
    uf = u.astype(jnp.float32)
    cs = jnp.pad(jnp.cumsum(uf, axis=1), ((0, 0), (1, 0), (0, 0)))
    count = jnp.arange(1, L + 1, dtype=jnp.float32)[None, :, None]
    means = []
    for gi, w in enumerate(POOL_WINDOWS):
        c = cs[..., gi * POOL_GROUP:(gi + 1) * POOL_GROUP]
        lo = jnp.pad(c[:, :L + 1 - w], ((0, 0), (w - 1, 0), (0, 0)))
        means.append((c[:, 1:] - lo) / jnp.minimum(count, float(w)))
    pooled = jnp.concatenate(means, axis=-1) - uf
    pooled = pooled.reshape(Bsz, L, len(POOL_WINDOWS), POOL_GROUP)
    y = jnp.einsum('blgc,gcd->blgd', pooled, w_group).reshape(Bsz, L, POOL_WIDTH)
    return (y * scale).astype(u.dtype)


def stick_breaking(q, k, v):
    L = q.shape[2]
    scale = SB_HEAD_DIM ** -0.5
    outs = []
    for i in range(L // SB_BLOCK):
        q0 = i * SB_BLOCK
        kend = q0 + SB_BLOCK
        qb = q[:, :, q0:kend]
        kb = k[:, :, :kend]
        vb = v[:, :, :kend]
        z = jnp.einsum('bhqd,bhkd->bhqk', qb, kb).astype(jnp.float32) * scale
        t_pos = q0 + jnp.arange(SB_BLOCK)[:, None]
        s_pos = jnp.arange(kend)[None, :]
        mask = s_pos < t_pos
        log_keep = jnp.where(mask, jax.nn.log_sigmoid(-z), 0.0)
        later = lax.cumsum(log_keep, axis=3, reverse=True) - log_keep
        w = jnp.where(mask, jnp.exp(jax.nn.log_sigmoid(z) + later), 0.0)
        outs.append(jnp.einsum('bhqk,bhkd->bhqd', w.astype(vb.dtype), vb))
    return jnp.concatenate(outs, axis=2)


def even_mixer(h, w_in, pool_w, pool_scale, w_out):
    Bsz, L, _ = h.shape
    p = h @ w_in
    u_pool = p[..., :POOL_WIDTH]
    qkv = p[..., POOL_WIDTH:].reshape(Bsz, L, 3, SB_HEADS, SB_HEAD_DIM)
    q = jnp.transpose(qkv[:, :, 0], (0, 2, 1, 3))
    k = jnp.transpose(qkv[:, :, 1], (0, 2, 1, 3))
    v = jnp.transpose(qkv[:, :, 2], (0, 2, 1, 3))
    y_pool = pool_mixer(u_pool, pool_w, pool_scale)
    y_sb = jnp.transpose(stick_breaking(q, k, v), (0, 2, 1, 3)).reshape(Bsz, L, SB_WIDTH)
    return jnp.concatenate([y_pool, y_sb.astype(y_pool.dtype)], axis=-1) @ w_out


def s5_ssm(u, a_re, a_im, log_dt, b_re, b_im, c_re, c_im, d_skip, glu_w, glu_b):
    Bsz, L, _ = u.shape
    f32 = jnp.float32
    uf = u.astype(f32).reshape(Bsz, L, SSM_GROUPS, SSM_GROUP)
    a_re = a_re.astype(f32)
    a_im = a_im.astype(f32)
    b_re = b_re.astype(f32)
    b_im = b_im.astype(f32)
    dt = jnp.exp(log_dt.astype(f32))[:, None]
    mag = jnp.exp(a_re * dt)
    lam_re = mag * jnp.cos(a_im * dt)
    lam_im = mag * jnp.sin(a_im * dt)
    den = a_re * a_re + a_im * a_im
    nr = lam_re - 1.0
    f_re = (nr * a_re + lam_im * a_im) / den
    f_im = (lam_im * a_re - nr * a_im) / den
    bb_re = f_re[..., None] * b_re - f_im[..., None] * b_im
    bb_im = f_re[..., None] * b_im + f_im[..., None] * b_re
    bu_re = jnp.einsum('gnc,blgc->lbgn', bb_re, uf)
    bu_im = jnp.einsum('gnc,blgc->lbgn', bb_im, uf)
    lr = jnp.broadcast_to(lam_re[None, None], (L, 1, SSM_GROUPS, SSM_STATE))
    li = jnp.broadcast_to(lam_im[None, None], (L, 1, SSM_GROUPS, SSM_STATE))

    def combine(left, right):
        ar1, ai1, br1, bi1 = left
        ar2, ai2, br2, bi2 = right
        return (ar2 * ar1 - ai2 * ai1, ar2 * ai1 + ai2 * ar1,
                ar2 * br1 - ai2 * bi1 + br2, ar2 * bi1 + ai2 * br1 + bi2)

    _, _, s_re, s_im = lax.associative_scan(combine, (lr, li, bu_re, bu_im), axis=0)
    y = (jnp.einsum('gcn,lbgn->blgc', c_re.astype(f32), s_re)
         - jnp.einsum('gcn,lbgn->blgc', c_im.astype(f32), s_im))
    y = y.reshape(Bsz, L, SSM_WIDTH) + d_skip.astype(f32) * uf.reshape(Bsz, L, SSM_WIDTH)
    y = jax.nn.gelu(y)
    y = y * jax.nn.sigmoid(y @ glu_w.astype(f32) + glu_b.astype(f32))
    return y.astype(u.dtype)


def spatial_gating(z, norm_g, w_s, b_s):
    Bsz, L, _ = z.shape
    z = jax.nn.gelu(z)
    u = z[..., :SGU_WIDTH]
    v = rmsnorm(z[..., SGU_WIDTH:], norm_g)
    v = v.reshape(Bsz, L // SGU_LEN, SGU_LEN, SGU_HEADS, SGU_HEAD_DIM)
    w_causal = jnp.tril(w_s)
    mixed = jnp.einsum('hts,bnshd->bnthd', w_causal, v) + jnp.transpose(b_s)[None, None, :, :, None]
    u = u.reshape(Bsz, L // SGU_LEN, SGU_LEN, SGU_HEADS, SGU_HEAD_DIM)
    return (u * mixed).reshape(Bsz, L, SGU_WIDTH)


def odd_mixer(h, w_in, a_re, a_im, log_dt, b_re, b_im, c_re, c_im, d_skip, glu_w, glu_b,
              sgu_norm_g, sgu_w, sgu_b, w_out):
    p = h @ w_in
    y_ssm = s5_ssm(p[..., :SSM_WIDTH], a_re, a_im, log_dt, b_re, b_im, c_re, c_im,
                   d_skip, glu_w, glu_b)
    y_sgu = spatial_gating(p[..., SSM_WIDTH:], sgu_norm_g, sgu_w, sgu_b)
    return jnp.concatenate([y_ssm, y_sgu.astype(y_ssm.dtype)], axis=-1) @ w_out


def setup_inputs(seed: int = 0) -> dict:
    key = jax.random.key(seed)
    ks = jax.random.split(key, 24)
    nrm = jax.random.normal
    f32 = jnp.float32
    G, N, C = SSM_GROUPS, SSM_STATE, SSM_GROUP
    x = nrm(ks[0], (BATCH, SEQ, D_MODEL), f32)
    norm_g = 1.0 + 0.02 * nrm(ks[1], (DEPTH, 6, D_MODEL), f32)
    ffn_w_gate = nrm(ks[2], (DEPTH, 2, D_MODEL, D_FF), f32) * D_MODEL ** -0.5
    ffn_w_up = nrm(ks[3], (DEPTH, 2, D_MODEL, D_FF), f32) * D_MODEL ** -0.5
    ffn_w_down = nrm(ks[4], (DEPTH, 2, D_FF, D_MODEL), f32) * D_FF ** -0.5
    ev_w_in = nrm(ks[5], (N_EVEN, D_MODEL, EVEN_IN), f32) * D_MODEL ** -0.5
    ev_pool_w = nrm(ks[6], (N_EVEN, len(POOL_WINDOWS), POOL_GROUP, POOL_GROUP), f32) * POOL_GROUP ** -0.5
    ev_pool_scale = 1.0 + 0.02 * nrm(ks[7], (N_EVEN, POOL_WIDTH), f32)
    ev_w_out = nrm(ks[8], (N_EVEN, EVEN_MIX, D_MODEL), f32) * EVEN_MIX ** -0.5
    od_w_in = nrm(ks[9], (N_ODD, D_MODEL, ODD_IN), f32) * D_MODEL ** -0.5
    od_ssm_a_re = -0.5 + 0.01 * nrm(ks[10], (N_ODD, G, N), f32)
    od_ssm_a_im = (math.pi * jnp.arange(N, dtype=f32))[None, None, :] + 0.01 * nrm(ks[11], (N_ODD, G, N), f32)
    od_ssm_log_dt = jax.random.uniform(ks[12], (N_ODD, G), f32, math.log(1e-3), math.log(1e-1))
    od_ssm_b_re = nrm(ks[13], (N_ODD, G, N, C), f32) * (2.0 * C) ** -0.5
    od_ssm_b_im = nrm(ks[14], (N_ODD, G, N, C), f32) * (2.0 * C) ** -0.5
    od_ssm_c_re = nrm(ks[15], (N_ODD, G, C, N), f32) * (2.0 * N) ** -0.5
    od_ssm_c_im = nrm(ks[16], (N_ODD, G, C, N), f32) * (2.0 * N) ** -0.5
    od_ssm_d = nrm(ks[17], (N_ODD, SSM_WIDTH), f32)
    od_glu_w = nrm(ks[18], (N_ODD, SSM_WIDTH, SSM_WIDTH), f32) * SSM_WIDTH ** -0.5
    od_glu_b = 0.01 * nrm(ks[19], (N_ODD, SSM_WIDTH), f32)
    od_sgu_norm_g = 1.0 + 0.02 * nrm(ks[20], (N_ODD, SGU_WIDTH), f32)
    od_sgu_w = nrm(ks[21], (N_ODD, SGU_HEADS, SGU_LEN, SGU_LEN), f32) * SGU_LEN ** -0.5
    od_sgu_b = 1.0 + 0.02 * nrm(ks[22], (N_ODD, SGU_HEADS, SGU_LEN), f32)
    od_w_out = nrm(ks[23], (N_ODD, ODD_MIX, D_MODEL), f32) * ODD_MIX ** -0.5
    return {'x': x, 'norm_g': norm_g, 'ffn_w_gate': ffn_w_gate, 'ffn_w_up': ffn_w_up,
            'ffn_w_down': ffn_w_down, 'ev_w_in': ev_w_in, 'ev_pool_w': ev_pool_w,
            'ev_pool_scale': ev_pool_scale, 'ev_w_out': ev_w_out, 'od_w_in': od_w_in,
            'od_ssm_a_re': od_ssm_a_re, 'od_ssm_a_im': od_ssm_a_im, 'od_ssm_log_dt': od_ssm_log_dt,
            'od_ssm_b_re': od_ssm_b_re, 'od_ssm_b_im': od_ssm_b_im, 'od_ssm_c_re': od_ssm_c_re,
            'od_ssm_c_im': od_ssm_c_im, 'od_ssm_d': od_ssm_d, 'od_glu_w': od_glu_w,
            'od_glu_b': od_glu_b, 'od_sgu_norm_g': od_sgu_norm_g, 'od_sgu_w': od_sgu_w,
            'od_sgu_b': od_sgu_b, 'od_w_out': od_w_out}


def reference(x, norm_g, ffn_w_gate, ffn_w_up, ffn_w_down, ev_w_in, ev_pool_w, ev_pool_scale,
              ev_w_out, od_w_in, od_ssm_a_re, od_ssm_a_im, od_ssm_log_dt, od_ssm_b_re, od_ssm_b_im,
              od_ssm_c_re, od_ssm_c_im, od_ssm_d, od_glu_w, od_glu_b, od_sgu_norm_g, od_sgu_w,
              od_sgu_b, od_w_out):
    for i in range(DEPTH):
        g = norm_g[i]
        f = swiglu(rmsnorm(x, g[0]), ffn_w_gate[i, 0], ffn_w_up[i, 0], ffn_w_down[i, 0])
        x = x + 0.5 * rmsnorm(f, g[1])
        h = rmsnorm(x, g[2])
        if i % 2 == 0:
            j = i // 2
            m = even_mixer(h, ev_w_in[j], ev_pool_w[j], ev_pool_scale[j], ev_w_out[j])
        else:
            j = i // 2
            m = odd_mixer(h, od_w_in[j], od_ssm_a_re[j], od_ssm_a_im[j], od_ssm_log_dt[j],
                          od_ssm_b_re[j], od_ssm_b_im[j], od_ssm_c_re[j], od_ssm_c_im[j],
                          od_ssm_d[j], od_glu_w[j], od_glu_b[j], od_sgu_norm_g[j],
                          od_sgu_w[j], od_sgu_b[j], od_w_out[j])
        x = x + rmsnorm(m.astype(x.dtype), g[3])
        f = swiglu(rmsnorm(x, g[4]), ffn_w_gate[i, 1], ffn_w_up[i, 1], ffn_w_down[i, 1])
        x = x + 0.5 * rmsnorm(f, g[5])
    return x
```

```python
import functools
import math

import jax
import jax.numpy as jnp
from jax import lax
from jax.experimental import pallas as pl
from jax.experimental.pallas import tpu as pltpu

F32 = jnp.float32
BF16 = jnp.bfloat16

LANES = 128
EPS = 1e-6
POOL_WINDOWS = (2, 4, 8, 16)
POOL_HALO = 16
N_POOL_BLOCKS = 4
SB_HEADS = 12
SGU_HEADS = 8
SSM_BLOCKS = 8
SSM_GROUP = 16
SSM_STATE = 64
GROUPS_PER_BLOCK = LANES // SSM_GROUP
CHUNK = 16
VMEM_LIMIT = 56 * 1024 * 1024


def _params(*sem):
    return pltpu.CompilerParams(dimension_semantics=sem, vmem_limit_bytes=VMEM_LIMIT)


def _rms(x, g):
    return x * lax.rsqrt(jnp.mean(x * x, axis=-1, keepdims=True) + EPS) * g


def _gelu(x):
    c = math.sqrt(2.0 / math.pi)
    return x * (0.5 * (1.0 + jnp.tanh(c * (x + 0.044715 * (x * x * x)))))


def _sigmoid(x):
    return 1.0 / (1.0 + jnp.exp(-x))


def _ffn_kernel(x_ref, gin_ref, wg_ref, wu_ref, wd_ref, gout_ref, o_ref, hn_ref, *, nf):
    f = pl.program_id(1)

    @pl.when(f == 0)
    def _():
        hn_ref[...] = _rms(x_ref[...], gin_ref[...]).astype(BF16)

    h = hn_ref[...]
    g = jnp.dot(h, wg_ref[...], preferred_element_type=F32)
    u = jnp.dot(h, wu_ref[...], preferred_element_type=F32)
    a = (g * _sigmoid(g) * u).astype(BF16)
    part = jnp.dot(a, wd_ref[...], preferred_element_type=F32)

    @pl.when(f == 0)
    def _():
        o_ref[...] = part

    @pl.when(f > 0)
    def _():
        o_ref[...] += part

    @pl.when(f == nf - 1)
    def _():
        o_ref[...] = x_ref[...] + 0.5 * _rms(o_ref[...], gout_ref[...])


def _ffn(x, g_in, wg, wu, wd, g_out, *, tm=512, tf=512):
    T, D = x.shape
    F = wg.shape[1]
    tm = min(tm, T)
    nf = F // tf
    return pl.pallas_call(
        functools.partial(_ffn_kernel, nf=nf),
        grid=(T // tm, nf),
        in_specs=[
            pl.BlockSpec((tm, D), lambda i, f: (i, 0)),
            pl.BlockSpec((1, D), lambda i, f: (0, 0)),
            pl.BlockSpec((D, tf), lambda i, f: (0, f)),
            pl.BlockSpec((D, tf), lambda i, f: (0, f)),
            pl.BlockSpec((tf, D), lambda i, f: (f, 0)),
            pl.BlockSpec((1, D), lambda i, f: (0, 0)),
        ],
        out_specs=pl.BlockSpec((tm, D), lambda i, f: (i, 0)),
        out_shape=jax.ShapeDtypeStruct((T, D), F32),
        scratch_shapes=[pltpu.VMEM((tm, D), BF16)],
        compiler_params=_params("parallel", "arbitrary"),
        name="ffn",
    )(x, g_in, wg, wu, wd, g_out)


def _proj_kernel(x_ref, g_ref, w_ref, o_ref, hn_ref, *, nb):
    @pl.when(pl.program_id(1) == 0)
    def _():
        hn_ref[...] = _rms(x_ref[...], g_ref[...]).astype(BF16)

    res = jnp.dot(hn_ref[...], w_ref[...], preferred_element_type=F32)
    for c in range(nb):
        o_ref[c] = res[:, c * LANES:(c + 1) * LANES].astype(o_ref.dtype)


def _norm_proj(x, g, w, *, tm=512, tn=512):
    T, D = x.shape
    N = w.shape[1]
    tm = min(tm, T)
    nb = tn // LANES
    return pl.pallas_call(
        functools.partial(_proj_kernel, nb=nb),
        grid=(T // tm, N // tn),
        in_specs=[
            pl.BlockSpec((tm, D), lambda i, j: (i, 0)),
            pl.BlockSpec((1, D), lambda i, j: (0, 0)),
            pl.BlockSpec((D, tn), lambda i, j: (0, j)),
        ],
        out_specs=pl.BlockSpec((nb, tm, LANES), lambda i, j: (j, i, 0)),
        out_shape=jax.ShapeDtypeStruct((N // LANES, T, LANES), BF16),
        scratch_shapes=[pltpu.VMEM((tm, D), BF16)],
        compiler_params=_params("parallel", "arbitrary"),
        name="norm_proj",
    )(x, g, w)


def _sb_kernel(q_ref, k_ref, v_ref, o_ref, *, tq, scale):
    qi = pl.program_id(2)
    q = q_ref[...]
    row = lax.broadcasted_iota(jnp.int32, (tq, tq), 0)
    col = lax.broadcasted_iota(jnp.int32, (tq, tq), 1)
    suffix = (row > col).astype(BF16)
    causal = col < row

    def tile(j, run, acc, diagonal):
        start = pl.multiple_of(j * tq, tq)
        kj = k_ref[pl.ds(start, tq), :]
        vj = v_ref[pl.ds(start, tq), :]
        z = lax.dot_general(q, kj, (((1,), (1,)), ((), ())), preferred_element_type=F32) * scale
        log_beta = jnp.minimum(z, 0.0) - jnp.log(1.0 + jnp.exp(-jnp.abs(z)))
        log_keep = log_beta - z
        if diagonal:
            log_keep = jnp.where(causal, log_keep, 0.0)
        hi = log_keep.astype(BF16)
        lo = (log_keep - hi.astype(F32)).astype(BF16)
        later = (jnp.dot(hi, suffix, preferred_element_type=F32)
                 + jnp.dot(lo, suffix, preferred_element_type=F32))
        w = jnp.exp(log_beta + later + run)
        if diagonal:
            w = jnp.where(causal, w, 0.0)
        acc = acc + jnp.dot(w.astype(BF16), vj, preferred_element_type=F32)
        run = run + jnp.sum(log_keep, axis=1, keepdims=True)
        return run, acc

    run0 = jnp.zeros((tq, 1), F32)
    acc0 = jnp.zeros((tq, LANES), F32)
    run, acc = tile(qi, run0, acc0, True)

    def body(i, carry):
        return tile(qi - 1 - i, carry[0], carry[1], False)

    run, acc = lax.fori_loop(0, qi, body, (run, acc))
    o_ref[...] = acc.astype(o_ref.dtype)


def _stick_breaking(p, B, L, *, tq=256):
    T = p.shape[1]
    tq = min(tq, L)
    nq = L // tq
    q0, k0, v0 = N_POOL_BLOCKS, N_POOL_BLOCKS + SB_HEADS, N_POOL_BLOCKS + 2 * SB_HEADS
    return pl.pallas_call(
        functools.partial(_sb_kernel, tq=tq, scale=LANES ** -0.5),
        grid=(B, SB_HEADS, nq),
        in_specs=[
            pl.BlockSpec((None, tq, LANES), lambda b, h, i: (q0 + h, b * nq + i, 0)),
            pl.BlockSpec((None, L, LANES), lambda b, h, i: (k0 + h, b, 0)),
            pl.BlockSpec((None, L, LANES), lambda b, h, i: (v0 + h, b, 0)),
        ],
        out_specs=pl.BlockSpec((None, tq, LANES), lambda b, h, i: (h, b * nq + i, 0)),
        out_shape=jax.ShapeDtypeStruct((SB_HEADS, T, LANES), BF16),
        compiler_params=_params("parallel", "parallel", "arbitrary"),
        name="stick_breaking",
    )(p, p, p)


def _even_out_kernel(x_ref, up_ref, halo_ref, ysb_ref, pw_ref, ps_ref, wo_ref, g_ref, o_ref,
                     uext_ref, mix_ref, *, tm, L):
    t0 = (pl.program_id(0) * tm) % L
    tpos = t0 + lax.broadcasted_iota(jnp.int32, (tm, 1), 0)
    for gi, win in enumerate(POOL_WINDOWS):
        u = up_ref[gi].astype(F32)
        halo = halo_ref[gi].astype(F32)
        uext_ref[0:POOL_HALO, :] = jnp.where(t0 == 0, 0.0, halo)
        uext_ref[POOL_HALO:, :] = u
        s = u
        for d in range(1, win):
            s = s + uext_ref[POOL_HALO - d:POOL_HALO - d + tm, :]
        count = jnp.minimum((tpos + 1).astype(F32), float(win))
        pooled = s / count - u
        y = jnp.dot(pooled.astype(BF16), pw_ref[gi], preferred_element_type=F32)
        y = y * ps_ref[:, gi * LANES:(gi + 1) * LANES]
        mix_ref[:, gi * LANES:(gi + 1) * LANES] = y.astype(BF16)
    for h in range(SB_HEADS):
        c0 = (N_POOL_BLOCKS + h) * LANES
        mix_ref[:, c0:c0 + LANES] = ysb_ref[h]
    m = jnp.dot(mix_ref[...], wo_ref[...], preferred_element_type=F32)
    o_ref[...] = x_ref[...] + _rms(m, g_ref[...])


def _even_out(x, p, ysb, pool_w, pool_scale, w_out, g, L, *, tm=512):
    T, D = x.shape
    tm = min(tm, L)
    hb = tm // POOL_HALO
    nmix = w_out.shape[0]
    return pl.pallas_call(
        functools.partial(_even_out_kernel, tm=tm, L=L),
        grid=(T // tm,),
        in_specs=[
            pl.BlockSpec((tm, D), lambda i: (i, 0)),
            pl.BlockSpec((N_POOL_BLOCKS, tm, LANES), lambda i: (0, i, 0)),
            pl.BlockSpec((N_POOL_BLOCKS, POOL_HALO, LANES), lambda i: (0, jnp.maximum(i * hb - 1, 0), 0)),
            pl.BlockSpec((SB_HEADS, tm, LANES), lambda i: (0, i, 0)),
            pl.BlockSpec(pool_w.shape, lambda i: (0, 0, 0)),
            pl.BlockSpec((1, N_POOL_BLOCKS * LANES), lambda i: (0, 0)),
            pl.BlockSpec((nmix, D), lambda i: (0, 0)),
            pl.BlockSpec((1, D), lambda i: (0, 0)),
        ],
        out_specs=pl.BlockSpec((tm, D), lambda i: (i, 0)),
        out_shape=jax.ShapeDtypeStruct((T, D), F32),
        scratch_shapes=[pltpu.VMEM((tm + POOL_HALO, LANES), F32), pltpu.VMEM((tm, nmix), BF16)],
        compiler_params=_params("parallel"),
        name="even_out",
    )(x, p, p, ysb, pool_w, pool_scale, w_out, g)


def _ssm_prep_kernel(are_ref, aim_ref, ldt_ref, bre_ref, bim_ref, cre_ref, cim_ref,
                     wre_ref, wim_ref, qre_ref, qim_ref, kd_ref, lre_ref, lim_ref):
    dt = jnp.exp(ldt_ref[...])
    are, aim = are_ref[...], aim_ref[...]

    def lam_pow(k):
        mag = jnp.exp(float(k) * are * dt)
        ang = float(k) * aim * dt
        return mag * jnp.cos(ang), mag * jnp.sin(ang)

    pows = [lam_pow(k) for k in range(CHUNK + 1)]
    lam_re, lam_im = pows[1]
    den = are * are + aim * aim
    nr = lam_re - 1.0
    f_re = (nr * are + lam_im * aim) / den
    f_im = (lam_im * are - nr * aim) / den
    b_re, b_im = bre_ref[...], bim_ref[...]
    bb_re = f_re * b_re - f_im * b_im
    bb_im = f_re * b_im + f_im * b_re
    c_re, c_im = cre_ref[...], cim_ref[...]
    lre_ref[...], lim_ref[...] = pows[CHUNK]

    for s in range(CHUNK):
        pr, pi = pows[CHUNK - 1 - s]
        wre_ref[:, s] = pr * bb_re - pi * bb_im
        wim_ref[:, s] = pr * bb_im + pi * bb_re
    for t in range(CHUNK):
        pr, pi = pows[t + 1]
        qre_ref[:, t] = c_re * pr - c_im * pi
        qim_ref[:, t] = -(c_re * pi + c_im * pr)
    for d in range(CHUNK):
        pr, pi = pows[d]
        cl_re = c_re * pr - c_im * pi
        cl_im = c_re * pi + c_im * pr
        kd = (jnp.einsum('gxn,gcn->gxc', cl_re, bb_re, preferred_element_type=F32,
                         precision=lax.Precision.HIGHEST)
              - jnp.einsum('gxn,gcn->gxc', cl_im, bb_im, preferred_element_type=F32,
                           precision=lax.Precision.HIGHEST))
        kd_ref[:, d] = kd


def _ssm_prep(a_re, a_im, log_dt, b_re, b_im, c_re, c_im, *, gb=8):
    G, N = a_re.shape
    C = b_re.shape[2]
    v3 = lambda a: a.reshape(G, 1, -1)
    bt = lambda a: jnp.transpose(a, (0, 2, 1))
    spec3 = lambda s1, s2: pl.BlockSpec((gb, s1, s2), lambda i: (i, 0, 0))
    spec4 = lambda s1, s2, s3: pl.BlockSpec((gb, s1, s2, s3), lambda i: (i, 0, 0, 0))
    out_shape = (
        jax.ShapeDtypeStruct((G, CHUNK, C, N), F32),
        jax.ShapeDtypeStruct((G, CHUNK, C, N), F32),
        jax.ShapeDtypeStruct((G, CHUNK, C, N), F32),
        jax.ShapeDtypeStruct((G, CHUNK, C, N), F32),
        jax.ShapeDtypeStruct((G, CHUNK, C, C), F32),
        jax.ShapeDtypeStruct((G, 1, N), F32),
        jax.ShapeDtypeStruct((G, 1, N), F32),
    )
    return pl.pallas_call(
        _ssm_prep_kernel,
        grid=(G // gb,),
        in_specs=[spec3(1, N), spec3(1, N), spec3(1, 1), spec3(C, N), spec3(C, N), spec3(C, N), spec3(C, N)],
        out_specs=(spec4(CHUNK, C, N), spec4(CHUNK, C, N), spec4(CHUNK, C, N), spec4(CHUNK, C, N),
                   spec4(CHUNK, C, C), spec3(1, N), spec3(1, N)),
        out_shape=out_shape,
        compiler_params=_params("parallel"),
        name="ssm_prep",
    )(v3(a_re), v3(a_im), log_dt.reshape(G, 1, 1), bt(b_re), bt(b_im), c_re, c_im)


def _ssm_block_weights(w_re, w_im, q_re, q_im, kd, l_re, l_im):
    G = w_re.shape[0]
    J, P, C, N = G // GROUPS_PER_BLOCK, GROUPS_PER_BLOCK, SSM_GROUP, SSM_STATE
    eye = jnp.eye(P, dtype=F32)
    w = jnp.stack([w_re, w_im], axis=3).reshape(J, P, CHUNK, C, 2, N)
    wcat = jnp.einsum('jgsirn,gh->jsgirhn', w, eye).reshape(J, CHUNK * P * C, 2 * P * N)
    q = jnp.stack([q_re, q_im], axis=1).reshape(J, P, 2, CHUNK, C, N)
    qcat = jnp.einsum('jgrtcn,gh->jrgnthc', q, eye).reshape(J, 2 * P * N, CHUNK * P * C)
    k = kd.reshape(J, P, CHUNK, C, C)
    kblk = jnp.einsum('jgdci,gh->jdgihc', k, eye).reshape(J, CHUNK, P * C, P * C)
    lam = jnp.concatenate([l_re.reshape(J, 1, P * N), l_im.reshape(J, 1, P * N)], axis=-1)
    return wcat.astype(BF16), qcat.astype(BF16), kblk.astype(BF16), lam


def _ssm_state_kernel(u_ref, w_ref, lam_ref, xs_ref, v_ref, xf_ref, *, B, nch, rt):
    R = B * nch
    half = GROUPS_PER_BLOCK * SSM_STATE
    for r in range(R // rt):
        v_ref[r * rt:(r + 1) * rt, :] = jnp.dot(u_ref[r * rt:(r + 1) * rt, :], w_ref[...],
                                                preferred_element_type=F32)
    lr, li = lam_ref[:, :half], lam_ref[:, half:]

    def body(n, carry):
        out = []
        for b in range(B):
            xr, xi = carry[2 * b], carry[2 * b + 1]
            row = b * nch + n
            xf_ref[pl.ds(row, 1), :half] = xr
            xf_ref[pl.ds(row, 1), half:] = xi
            v = v_ref[pl.ds(row, 1), :]
            out.append(lr * xr - li * xi + v[:, :half])
            out.append(lr * xi + li * xr + v[:, half:])
        return tuple(out)

    lax.fori_loop(0, nch, body, tuple(jnp.zeros((1, half), F32) for _ in range(2 * B)))
    xs_ref[...] = xf_ref[...].astype(BF16)


def _ssm_out_kernel(u_ref, xs_ref, kd_ref, q_ref, y_ref, toep_ref, *, R, rt):
    @pl.when(pl.program_id(0) == 0)
    def _():
        toep_ref[...] = jnp.zeros_like(toep_ref)

    for s in range(CHUNK):
        for t in range(s, CHUNK):
            toep_ref[s * LANES:(s + 1) * LANES, t * LANES:(t + 1) * LANES] = kd_ref[t - s]
    for r in range(R // rt):
        rows = slice(r * rt, (r + 1) * rt)
        y_ref[rows, :] = (jnp.dot(u_ref[rows, :], toep_ref[...], preferred_element_type=F32)
                          + jnp.dot(xs_ref[rows, :], q_ref[...], preferred_element_type=F32))


def _ssm(p, wcat, qcat, kblk, lam, B, L):
    T = p.shape[1]
    R = T // CHUNK
    nch = L // CHUNK
    rt = min(256, R)
    wide = CHUNK * LANES
    nstate = 2 * GROUPS_PER_BLOCK * SSM_STATE
    u2 = p.reshape(p.shape[0], R, wide)
    xs = pl.pallas_call(
        functools.partial(_ssm_state_kernel, B=B, nch=nch, rt=rt),
        grid=(SSM_BLOCKS,),
        in_specs=[
            pl.BlockSpec((None, R, wide), lambda j: (j, 0, 0)),
            pl.BlockSpec((None, wide, nstate), lambda j: (j, 0, 0)),
            pl.BlockSpec((None, 1, nstate), lambda j: (j, 0, 0)),
        ],
        out_specs=pl.BlockSpec((None, R, nstate), lambda j: (j, 0, 0)),
        out_shape=jax.ShapeDtypeStruct((SSM_BLOCKS, R, nstate), BF16),
        scratch_shapes=[pltpu.VMEM((R, nstate), F32), pltpu.VMEM((R, nstate), F32)],
        compiler_params=_params("parallel"),
        name="ssm_state",
    )(u2, wcat, lam)
    y = pl.pallas_call(
        functools.partial(_ssm_out_kernel, R=R, rt=rt),
        grid=(SSM_BLOCKS,),
        in_specs=[
            pl.BlockSpec((None, R, wide), lambda j: (j, 0, 0)),
            pl.BlockSpec((None, R, nstate), lambda j: (j, 0, 0)),
            pl.BlockSpec((None, CHUNK, LANES, LANES), lambda j: (j, 0, 0, 0)),
            pl.BlockSpec((None, nstate, wide), lambda j: (j, 0, 0)),
        ],
        out_specs=pl.BlockSpec((None, R, wide), lambda j: (j, 0, 0)),
        out_shape=jax.ShapeDtypeStruct((SSM_BLOCKS, R, wide), F32),
        scratch_shapes=[pltpu.VMEM((wide, wide), BF16)],
        compiler_params=_params("arbitrary"),
        name="ssm_out",
    )(u2, xs, kblk, qcat)
    return y.reshape(SSM_BLOCKS, T, LANES)


def _odd_out_kernel(x_ref, y_ref, p_ref, dsk_ref, gw_ref, gb_ref, ng_ref, sw_ref, sbt_ref, wo_ref,
                    g_ref, o_ref, yt_ref, mix_ref, *, tm):
    nssm = SSM_BLOCKS * LANES
    for j in range(SSM_BLOCKS):
        cols = slice(j * LANES, (j + 1) * LANES)
        yt_ref[:, cols] = _gelu(y_ref[j] + dsk_ref[:, cols] * p_ref[j].astype(F32))
    y = yt_ref[...]
    gate = jnp.dot(y.astype(BF16), gw_ref[...], preferred_element_type=F32) + gb_ref[...]
    mix_ref[:, :nssm] = (y * _sigmoid(gate)).astype(BF16)

    u0, v0 = SSM_BLOCKS, SSM_BLOCKS + SGU_HEADS
    ssq = jnp.zeros((tm, 1), F32)
    for h in range(SGU_HEADS):
        zv = _gelu(p_ref[v0 + h].astype(F32))
        ssq = ssq + jnp.sum(zv * zv, axis=-1, keepdims=True)
    inv = lax.rsqrt(ssq / float(SGU_HEADS * LANES) + EPS)
    row = lax.broadcasted_iota(jnp.int32, (LANES, LANES), 0)
    col = lax.broadcasted_iota(jnp.int32, (LANES, LANES), 1)
    nblk = tm // LANES
    for h in range(SGU_HEADS):
        cols = slice(h * LANES, (h + 1) * LANES)
        v = (_gelu(p_ref[v0 + h].astype(F32)) * inv * ng_ref[:, cols]).astype(BF16)
        vcat = jnp.concatenate([v[r * LANES:(r + 1) * LANES, :] for r in range(nblk)], axis=1)
        w = jnp.where(col <= row, sw_ref[h], jnp.zeros((), BF16))
        mixed = jnp.dot(w, vcat, preferred_element_type=F32) + sbt_ref[:, h:h + 1]
        zu = _gelu(p_ref[u0 + h].astype(F32))
        for r in range(nblk):
            rows = slice(r * LANES, (r + 1) * LANES)
            mix_ref[rows, nssm + h * LANES:nssm + (h + 1) * LANES] = (
                zu[rows, :] * mixed[:, rows]).astype(BF16)
    m = jnp.dot(mix_ref[...], wo_ref[...], preferred_element_type=F32)
    o_ref[...] = x_ref[...] + _rms(m, g_ref[...])


def _odd_out(x, y8, p, d_skip, glu_w, glu_b, sgu_g, sgu_w, sgu_bt, w_out, g, *, tm=256):
    T, D = x.shape
    tm = min(tm, T)
    nmix = w_out.shape[0]
    nssm = SSM_BLOCKS * LANES
    full2 = lambda a: pl.BlockSpec(a.shape, lambda i: (0, 0))
    return pl.pallas_call(
        functools.partial(_odd_out_kernel, tm=tm),
        grid=(T // tm,),
        in_specs=[
            pl.BlockSpec((tm, D), lambda i: (i, 0)),
            pl.BlockSpec((SSM_BLOCKS, tm, LANES), lambda i: (0, i, 0)),
            pl.BlockSpec((p.shape[0], tm, LANES), lambda i: (0, i, 0)),
            full2(d_skip), full2(glu_w), full2(glu_b), full2(sgu_g),
            pl.BlockSpec(sgu_w.shape, lambda i: (0, 0, 0)),
            full2(sgu_bt), full2(w_out), full2(g),
        ],
        out_specs=pl.BlockSpec((tm, D), lambda i: (i, 0)),
        out_shape=jax.ShapeDtypeStruct((T, D), F32),
        scratch_shapes=[pltpu.VMEM((tm, nssm), F32), pltpu.VMEM((tm, nmix), BF16)],
        compiler_params=_params("parallel"),
        name="odd_out",
    )(x, y8, p, d_skip, glu_w, glu_b, sgu_g, sgu_w, sgu_bt, w_out, g)


def kernel(x, norm_g, ffn_w_gate, ffn_w_up, ffn_w_down, ev_w_in, ev_pool_w, ev_pool_scale, ev_w_out, od_w_in, od_ssm_a_re, od_ssm_a_im, od_ssm_log_dt, od_ssm_b_re, od_ssm_b_im, od_ssm_c_re, od_ssm_c_im, od_ssm_d, od_glu_w, od_glu_b, od_sgu_norm_g, od_sgu_w, od_sgu_b, od_w_out):
    B, L, D = x.shape
    depth = norm_g.shape[0]
    h = x.reshape(B * L, D)
    row = lambda a: a.reshape(1, -1)
    bf = lambda a: a.astype(BF16)
    for i in range(depth):
        g = norm_g[i]
        h = _ffn(h, row(g[0]), bf(ffn_w_gate[i, 0]), bf(ffn_w_up[i, 0]), bf(ffn_w_down[i, 0]), row(g[1]))
        j = i // 2
        if i % 2 == 0:
            p = _norm_proj(h, row(g[2]), bf(ev_w_in[j]))
            ysb = _stick_breaking(p, B, L)
            h = _even_out(h, p, ysb, bf(ev_pool_w[j]), row(ev_pool_scale[j]), bf(ev_w_out[j]), row(g[3]), L)
        else:
            p = _norm_proj(h, row(g[2]), bf(od_w_in[j]))
            prep = _ssm_prep(od_ssm_a_re[j], od_ssm_a_im[j], od_ssm_log_dt[j], od_ssm_b_re[j],
                             od_ssm_b_im[j], od_ssm_c_re[j], od_ssm_c_im[j])
            wcat, qcat, kblk, lam = _ssm_block_weights(*prep)
            y8 = _ssm(p, wcat, qcat, kblk, lam, B, L)
            h = _odd_out(h, y8, p, row(od_ssm_d[j]), bf(od_glu_w[j]), row(od_glu_b[j]),
                         row(od_sgu_norm_g[j]), bf(od_sgu_w[j]), jnp.transpose(od_sgu_b[j]),
                         bf(od_w_out[j]), row(g[3]))
        h = _ffn(h, row(g[4]), bf(ffn_w_gate[i, 1]), bf(ffn_w_up[i, 1]), bf(ffn_w_down[i, 1]), row(g[5]))
    return h.reshape(B, L, D)
```

```python
import functools
import math

import jax
import jax.numpy as jnp
from jax import lax
from jax.experimental import pallas as pl
from jax.experimental.pallas import tpu as pltpu

F32 = jnp.float32
BF16 = jnp.bfloat16

LANES = 128
EPS = 1e-6
POOL_WINDOWS = (2, 4, 8, 16)
POOL_HALO = 16
N_POOL_BLOCKS = 4
SB_HEADS = 12
SGU_HEADS = 8
SSM_BLOCKS = 8
SSM_GROUP = 16
SSM_STATE = 64
GROUPS_PER_BLOCK = LANES // SSM_GROUP
CHUNK = 16
VMEM_LIMIT = 56 * 1024 * 1024


def _params(*sem):
    return pltpu.CompilerParams(dimension_semantics=sem, vmem_limit_bytes=VMEM_LIMIT)


def _rms(x, g):
    return x * lax.rsqrt(jnp.mean(x * x, axis=-1, keepdims=True) + EPS) * g


def _gelu(x):
    c = math.sqrt(2.0 / math.pi)
    return x * (0.5 * (1.0 + jnp.tanh(c * (x + 0.044715 * (x * x * x)))))


def _sigmoid(x):
    return 1.0 / (1.0 + jnp.exp(-x))


def _ffn_kernel(x_ref, gin_ref, wg_ref, wu_ref, wd_ref, gout_ref, o_ref, hn_ref, *, nf):
    f = pl.program_id(1)

    @pl.when(f == 0)
    def _():
        hn_ref[...] = _rms(x_ref[...], gin_ref[...]).astype(BF16)
        o_ref[...] = jnp.zeros_like(o_ref)

    h = hn_ref[...]
    g = jnp.dot(h, wg_ref[...], preferred_element_type=F32)
    u = jnp.dot(h, wu_ref[...], preferred_element_type=F32)
    a = (g * _sigmoid(g) * u).astype(BF16)
    o_ref[...] += jnp.dot(a, wd_ref[...], preferred_element_type=F32)

    @pl.when(f == nf - 1)
    def _():
        o_ref[...] = x_ref[...] + 0.5 * _rms(o_ref[...], gout_ref[...])


def _ffn(x, g_in, wg, wu, wd, g_out, *, tm=512, tf=512):
    T, D = x.shape
    F = wg.shape[1]
    tm = min(tm, T)
    nf = F // tf
    return pl.pallas_call(
        functools.partial(_ffn_kernel, nf=nf),
        grid=(T // tm, nf),
        in_specs=[
            pl.BlockSpec((tm, D), lambda i, f: (i, 0)),
            pl.BlockSpec((1, D), lambda i, f: (0, 0)),
            pl.BlockSpec((D, tf), lambda i, f: (0, f)),
            pl.BlockSpec((D, tf), lambda i, f: (0, f)),
            pl.BlockSpec((tf, D), lambda i, f: (f, 0)),
            pl.BlockSpec((1, D), lambda i, f: (0, 0)),
        ],
        out_specs=pl.BlockSpec((tm, D), lambda i, f: (i, 0)),
        out_shape=jax.ShapeDtypeStruct((T, D), F32),
        scratch_shapes=[pltpu.VMEM((tm, D), BF16)],
        compiler_params=_params("parallel", "arbitrary"),
        name="ffn",
    )(x, g_in, wg, wu, wd, g_out)


def _proj_kernel(x_ref, g_ref, w_ref, *rest, nb, tm, chunked):
    if chunked:
        oc_ref, ob_ref, hn_ref, res_ref = rest
    else:
        ob_ref, hn_ref = rest
    j = pl.program_id(1)

    @pl.when(j == 0)
    def _():
        hn_ref[...] = _rms(x_ref[...], g_ref[...]).astype(BF16)

    res = jnp.dot(hn_ref[...], w_ref[...], preferred_element_type=F32)

    def store_blocked():
        for c in range(nb):
            ob_ref[c] = res[:, c * LANES:(c + 1) * LANES].astype(ob_ref.dtype)

    if not chunked:
        store_blocked()
        return

    @pl.when(j == 0)
    def _():
        for c in range(nb):
            res_ref[c] = res[:, c * LANES:(c + 1) * LANES]
        for c in range(nb):
            for s in range(CHUNK):
                piece = res_ref[c, pl.ds(s, tm // CHUNK, stride=CHUNK), :]
                oc_ref[c, :, s * LANES:(s + 1) * LANES] = piece.astype(oc_ref.dtype)

    pl.when(j > 0)(store_blocked)


def _norm_proj(x, g, w, *, chunked=False, tm=1024, tn=1024):
    T, D = x.shape
    N = w.shape[1]
    tm = min(tm, T)
    nb = tn // LANES
    nj = N // tn
    in_specs = [
        pl.BlockSpec((tm, D), lambda i, j: (i, 0)),
        pl.BlockSpec((1, D), lambda i, j: (0, 0)),
        pl.BlockSpec((D, tn), lambda i, j: (0, j)),
    ]
    scratch = [pltpu.VMEM((tm, D), BF16)]
    if chunked:
        out_specs = (pl.BlockSpec((nb, tm // CHUNK, CHUNK * LANES), lambda i, j: (0, i, 0)),
                     pl.BlockSpec((nb, tm, LANES), lambda i, j: (jnp.maximum(j - 1, 0), i, 0)))
        out_shape = (jax.ShapeDtypeStruct((nb, T // CHUNK, CHUNK * LANES), BF16),
                     jax.ShapeDtypeStruct(((nj - 1) * nb, T, LANES), BF16))
        scratch.append(pltpu.VMEM((nb, tm, LANES), F32))
    else:
        out_specs = pl.BlockSpec((nb, tm, LANES), lambda i, j: (j, i, 0))
        out_shape = jax.ShapeDtypeStruct((nj * nb, T, LANES), BF16)
    return pl.pallas_call(
        functools.partial(_proj_kernel, nb=nb, tm=tm, chunked=chunked),
        grid=(T // tm, nj),
        in_specs=in_specs,
        out_specs=out_specs,
        out_shape=out_shape,
        scratch_shapes=scratch,
        compiler_params=_params("parallel", "arbitrary"),
        name="norm_proj_chunked" if chunked else "norm_proj",
    )(x, g, w)


def _sb_kernel(q_ref, k_ref, v_ref, o_ref, run_ref, acc_ref, *, tq, tk, scale, hp):
    qi = pl.program_id(2)
    nsub = tq // tk
    krow = lax.broadcasted_iota(jnp.int32, (tk, tk), 0)
    kcol = lax.broadcasted_iota(jnp.int32, (tk, tk), 1)
    suffix = (krow > kcol).astype(BF16)
    run_ref[...] = jnp.zeros_like(run_ref)
    acc_ref[...] = jnp.zeros_like(acc_ref)

    def tile(h, j, r0):
        rows = slice(0 if r0 is None else r0, tq)
        m = tq - rows.start
        start = pl.multiple_of(j * tk, tk)
        kj = k_ref[h, pl.ds(start, tk), :]
        vj = v_ref[h, pl.ds(start, tk), :]
        z = lax.dot_general(q_ref[h, rows, :], kj, (((1,), (1,)), ((), ())),
                            preferred_element_type=F32) * scale
        log_beta = jnp.minimum(z, 0.0) - jnp.log(1.0 + jnp.exp(-jnp.abs(z)))
        log_keep = log_beta - z
        if r0 is not None:
            causal = (lax.broadcasted_iota(jnp.int32, (m, tk), 1)
                      < lax.broadcasted_iota(jnp.int32, (m, tk), 0))
            log_keep = jnp.where(causal, log_keep, 0.0)
        hi = log_keep.astype(BF16)
        lo = (log_keep - hi.astype(F32)).astype(BF16)
        both = jnp.dot(jnp.concatenate([hi, lo], axis=0), suffix, preferred_element_type=F32)
        later = both[:m] + both[m:]
        w = jnp.exp(log_beta + later + run_ref[h, rows, :])
        if r0 is not None:
            w = jnp.where(causal, w, 0.0)
        acc_ref[h, rows, :] += jnp.dot(w.astype(BF16), vj, preferred_element_type=F32)
        run_ref[h, rows, :] += jnp.sum(log_keep, axis=1, keepdims=True)

    for sub in reversed(range(nsub)):
        for h in range(hp):
            tile(h, qi * nsub + sub, sub * tk)

    def below(i, c):
        for h in range(hp):
            tile(h, qi * nsub - 1 - i, None)
        return c

    lax.fori_loop(0, qi * nsub, below, 0)
    for h in range(hp):
        o_ref[h] = acc_ref[h].astype(o_ref.dtype)


def _stick_breaking(p, B, L, *, tq=1024, tk=256, hp=4):
    T = p.shape[1]
    tq = min(tq, L)
    tk = min(tk, tq)
    nq = L // tq
    q0, k0, v0 = (blk // hp for blk in
                  (N_POOL_BLOCKS, N_POOL_BLOCKS + SB_HEADS, N_POOL_BLOCKS + 2 * SB_HEADS))
    return pl.pallas_call(
        functools.partial(_sb_kernel, tq=tq, tk=tk, scale=LANES ** -0.5, hp=hp),
        grid=(B, SB_HEADS // hp, nq),
        in_specs=[
            pl.BlockSpec((hp, tq, LANES), lambda b, h, i: (q0 + h, b * nq + i, 0)),
            pl.BlockSpec((hp, L, LANES), lambda b, h, i: (k0 + h, b, 0)),
            pl.BlockSpec((hp, L, LANES), lambda b, h, i: (v0 + h, b, 0)),
        ],
        out_specs=pl.BlockSpec((hp, tq, LANES), lambda b, h, i: (h, b * nq + i, 0)),
        out_shape=jax.ShapeDtypeStruct((SB_HEADS, T, LANES), BF16),
        scratch_shapes=[pltpu.VMEM((hp, tq, 1), F32), pltpu.VMEM((hp, tq, LANES), F32)],
        compiler_params=_params("parallel", "parallel", "arbitrary"),
        name="stick_breaking",
    )(p, p, p)


def _even_out_kernel(x_ref, up_ref, halo_ref, ysb_ref, pw_ref, ps_ref, wo_ref, g_ref, o_ref,
                     uext_ref, mix_ref, *, tm, L):
    t0 = (pl.program_id(0) * tm) % L
    tpos = t0 + lax.broadcasted_iota(jnp.int32, (tm, 1), 0)
    for gi, win in enumerate(POOL_WINDOWS):
        u = up_ref[gi].astype(F32)
        halo = halo_ref[gi].astype(F32)
        uext_ref[0:POOL_HALO, :] = jnp.where(t0 == 0, 0.0, halo)
        uext_ref[POOL_HALO:, :] = u
        s = u
        for d in range(1, win):
            s = s + uext_ref[POOL_HALO - d:POOL_HALO - d + tm, :]
        count = jnp.minimum((tpos + 1).astype(F32), float(win))
        pooled = s / count - u
        y = jnp.dot(pooled.astype(BF16), pw_ref[gi], preferred_element_type=F32)
        y = y * ps_ref[:, gi * LANES:(gi + 1) * LANES]
        mix_ref[:, gi * LANES:(gi + 1) * LANES] = y.astype(BF16)
    for h in range(SB_HEADS):
        c0 = (N_POOL_BLOCKS + h) * LANES
        mix_ref[:, c0:c0 + LANES] = ysb_ref[h]
    m = jnp.dot(mix_ref[...], wo_ref[...], preferred_element_type=F32)
    o_ref[...] = x_ref[...] + _rms(m, g_ref[...])


def _even_out(x, p, ysb, pool_w, pool_scale, w_out, g, L, *, tm=512):
    T, D = x.shape
    tm = min(tm, L)
    hb = tm // POOL_HALO
    nmix = w_out.shape[0]
    return pl.pallas_call(
        functools.partial(_even_out_kernel, tm=tm, L=L),
        grid=(T // tm,),
        in_specs=[
            pl.BlockSpec((tm, D), lambda i: (i, 0)),
            pl.BlockSpec((N_POOL_BLOCKS, tm, LANES), lambda i: (0, i, 0)),
            pl.BlockSpec((N_POOL_BLOCKS, POOL_HALO, LANES), lambda i: (0, jnp.maximum(i * hb - 1, 0), 0)),
            pl.BlockSpec((SB_HEADS, tm, LANES), lambda i: (0, i, 0)),
            pl.BlockSpec(pool_w.shape, lambda i: (0, 0, 0)),
            pl.BlockSpec((1, N_POOL_BLOCKS * LANES), lambda i: (0, 0)),
            pl.BlockSpec((nmix, D), lambda i: (0, 0)),
            pl.BlockSpec((1, D), lambda i: (0, 0)),
        ],
        out_specs=pl.BlockSpec((tm, D), lambda i: (i, 0)),
        out_shape=jax.ShapeDtypeStruct((T, D), F32),
        scratch_shapes=[pltpu.VMEM((tm + POOL_HALO, LANES), F32), pltpu.VMEM((tm, nmix), BF16)],
        compiler_params=_params("parallel"),
        name="even_out",
    )(x, p, p, ysb, pool_w, pool_scale, w_out, g)


def _ssm_prep_kernel(are_ref, aim_ref, ldt_ref, bre_ref, bim_ref, cre_ref, cim_ref,
                     wre_ref, wim_ref, qre_ref, qim_ref, kd_ref, lre_ref, lim_ref):
    dt = jnp.exp(ldt_ref[...])
    are, aim = are_ref[...], aim_ref[...]

    def lam_pow(k):
        mag = jnp.exp(float(k) * are * dt)
        ang = float(k) * aim * dt
        return mag * jnp.cos(ang), mag * jnp.sin(ang)

    pows = [lam_pow(k) for k in range(CHUNK + 1)]
    lam_re, lam_im = pows[1]
    den = are * are + aim * aim
    nr = lam_re - 1.0
    f_re = (nr * are + lam_im * aim) / den
    f_im = (lam_im * are - nr * aim) / den
    b_re, b_im = bre_ref[...], bim_ref[...]
    bb_re = f_re * b_re - f_im * b_im
    bb_im = f_re * b_im + f_im * b_re
    c_re, c_im = cre_ref[...], cim_ref[...]
    lre_ref[...], lim_ref[...] = pows[CHUNK]

    for s in range(CHUNK):
        pr, pi = pows[CHUNK - 1 - s]
        wre_ref[:, s] = pr * bb_re - pi * bb_im
        wim_ref[:, s] = pr * bb_im + pi * bb_re
    for t in range(CHUNK):
        pr, pi = pows[t + 1]
        qre_ref[:, t] = c_re * pr - c_im * pi
        qim_ref[:, t] = -(c_re * pi + c_im * pr)
    for d in range(CHUNK):
        pr, pi = pows[d]
        cl_re = c_re * pr - c_im * pi
        cl_im = c_re * pi + c_im * pr
        kd = (jnp.einsum('gxn,gcn->gxc', cl_re, bb_re, preferred_element_type=F32,
                         precision=lax.Precision.HIGHEST)
              - jnp.einsum('gxn,gcn->gxc', cl_im, bb_im, preferred_element_type=F32,
                           precision=lax.Precision.HIGHEST))
        kd_ref[:, d] = kd


def _ssm_prep(a_re, a_im, log_dt, b_re, b_im, c_re, c_im, *, gb=8):
    G, N = a_re.shape
    C = b_re.shape[2]
    v3 = lambda a: a.reshape(G, 1, -1)
    bt = lambda a: jnp.transpose(a, (0, 2, 1))
    spec3 = lambda s1, s2: pl.BlockSpec((gb, s1, s2), lambda i: (i, 0, 0))
    spec4 = lambda s1, s2, s3: pl.BlockSpec((gb, s1, s2, s3), lambda i: (i, 0, 0, 0))
    out_shape = (
        jax.ShapeDtypeStruct((G, CHUNK, C, N), F32),
        jax.ShapeDtypeStruct((G, CHUNK, C, N), F32),
        jax.ShapeDtypeStruct((G, CHUNK, C, N), F32),
        jax.ShapeDtypeStruct((G, CHUNK, C, N), F32),
        jax.ShapeDtypeStruct((G, CHUNK, C, C), F32),
        jax.ShapeDtypeStruct((G, 1, N), F32),
        jax.ShapeDtypeStruct((G, 1, N), F32),
    )
    return pl.pallas_call(
        _ssm_prep_kernel,
        grid=(G // gb,),
        in_specs=[spec3(1, N), spec3(1, N), spec3(1, 1), spec3(C, N), spec3(C, N), spec3(C, N), spec3(C, N)],
        out_specs=(spec4(CHUNK, C, N), spec4(CHUNK, C, N), spec4(CHUNK, C, N), spec4(CHUNK, C, N),
                   spec4(CHUNK, C, C), spec3(1, N), spec3(1, N)),
        out_shape=out_shape,
        compiler_params=_params("parallel"),
        name="ssm_prep",
    )(v3(a_re), v3(a_im), log_dt.reshape(G, 1, 1), bt(b_re), bt(b_im), c_re, c_im)


def _ssm_block_weights(w_re, w_im, q_re, q_im, kd, l_re, l_im):
    G = w_re.shape[0]
    J, P, C, N = G // GROUPS_PER_BLOCK, GROUPS_PER_BLOCK, SSM_GROUP, SSM_STATE
    eye = jnp.eye(P, dtype=F32)
    w = jnp.stack([w_re, w_im], axis=3).reshape(J, P, CHUNK, C, 2, N)
    wcat = jnp.einsum('jgsirn,gh->jsgirhn', w, eye).reshape(J, CHUNK * P * C, 2 * P * N)
    q = jnp.stack([q_re, q_im], axis=1).reshape(J, P, 2, CHUNK, C, N)
    qcat = jnp.einsum('jgrtcn,gh->jrgnthc', q, eye).reshape(J, 2 * P * N, CHUNK * P * C)
    k = kd.reshape(J, P, CHUNK, C, C)
    kblk = jnp.einsum('jgdci,gh->jdgihc', k, eye).reshape(J, CHUNK, P * C, P * C)
    lam = jnp.concatenate([l_re.reshape(J, 1, P * N), l_im.reshape(J, 1, P * N)], axis=-1)
    return wcat.astype(BF16), qcat.astype(BF16), kblk.astype(BF16), lam


def _ssm_state_kernel(u_ref, w_ref, lam_ref, xs_ref, v_ref, xf_ref, *, B, nch, rt):
    R = B * nch
    half = GROUPS_PER_BLOCK * SSM_STATE
    for r in range(R // rt):
        v_ref[r * rt:(r + 1) * rt, :] = jnp.dot(u_ref[r * rt:(r + 1) * rt, :], w_ref[...],
                                                preferred_element_type=F32)
    lr, li = lam_ref[:, :half], lam_ref[:, half:]

    def body(n, carry):
        out = []
        for b in range(B):
            xr, xi = carry[2 * b], carry[2 * b + 1]
            row = b * nch + n
            xf_ref[pl.ds(row, 1), :half] = xr
            xf_ref[pl.ds(row, 1), half:] = xi
            v = v_ref[pl.ds(row, 1), :]
            out.append(lr * xr - li * xi + v[:, :half])
            out.append(lr * xi + li * xr + v[:, half:])
        return tuple(out)

    lax.fori_loop(0, nch, body, tuple(jnp.zeros((1, half), F32) for _ in range(2 * B)))
    xs_ref[...] = xf_ref[...].astype(BF16)


def _ssm_out_kernel(u_ref, xs_ref, kd_ref, q_ref, dsk_ref, y_ref, toep_ref, *, R, rt):
    @pl.when(pl.program_id(0) == 0)
    def _():
        toep_ref[...] = jnp.zeros_like(toep_ref)

    for s in range(CHUNK):
        for t in range(s, CHUNK):
            toep_ref[s * LANES:(s + 1) * LANES, t * LANES:(t + 1) * LANES] = kd_ref[t - s]
    for r in range(R // rt):
        rows = slice(r * rt, (r + 1) * rt)
        u = u_ref[rows, :]
        y_ref[rows, :] = (jnp.dot(u, toep_ref[...], preferred_element_type=F32)
                          + jnp.dot(xs_ref[rows, :], q_ref[...], preferred_element_type=F32)
                          + dsk_ref[...] * u.astype(F32))


def _ssm(u2, wcat, qcat, kblk, lam, dsk, B):
    R = u2.shape[1]
    nch = R // B
    rt = min(256, R)
    wide = CHUNK * LANES
    nstate = 2 * GROUPS_PER_BLOCK * SSM_STATE
    xs = pl.pallas_call(
        functools.partial(_ssm_state_kernel, B=B, nch=nch, rt=rt),
        grid=(SSM_BLOCKS,),
        in_specs=[
            pl.BlockSpec((None, R, wide), lambda j: (j, 0, 0)),
            pl.BlockSpec((None, wide, nstate), lambda j: (j, 0, 0)),
            pl.BlockSpec((None, 1, nstate), lambda j: (j, 0, 0)),
        ],
        out_specs=pl.BlockSpec((None, R, nstate), lambda j: (j, 0, 0)),
        out_shape=jax.ShapeDtypeStruct((SSM_BLOCKS, R, nstate), BF16),
        scratch_shapes=[pltpu.VMEM((R, nstate), F32), pltpu.VMEM((R, nstate), F32)],
        compiler_params=_params("parallel"),
        name="ssm_state",
    )(u2, wcat, lam)
    return pl.pallas_call(
        functools.partial(_ssm_out_kernel, R=R, rt=rt),
        grid=(SSM_BLOCKS,),
        in_specs=[
            pl.BlockSpec((None, R, wide), lambda j: (j, 0, 0)),
            pl.BlockSpec((None, R, nstate), lambda j: (j, 0, 0)),
            pl.BlockSpec((None, CHUNK, LANES, LANES), lambda j: (j, 0, 0, 0)),
            pl.BlockSpec((None, nstate, wide), lambda j: (j, 0, 0)),
            pl.BlockSpec((None, 1, wide), lambda j: (j, 0, 0)),
        ],
        out_specs=pl.BlockSpec((None, R, wide), lambda j: (j, 0, 0)),
        out_shape=jax.ShapeDtypeStruct((SSM_BLOCKS, R, wide), F32),
        scratch_shapes=[pltpu.VMEM((wide, wide), BF16)],
        compiler_params=_params("arbitrary"),
        name="ssm_out",
    )(u2, xs, kblk, qcat, dsk)


def _odd_out_kernel(x_ref, y_ref, p_ref, gw_ref, gb_ref, ng_ref, sw_ref, sbt_ref, wo_ref,
                    g_ref, o_ref, yt_ref, mix_ref, *, tm):
    nssm = SSM_BLOCKS * LANES
    for j in range(SSM_BLOCKS):
        for s in range(CHUNK):
            yt_ref[j, pl.ds(s, tm // CHUNK, stride=CHUNK), :] = y_ref[j, :, s * LANES:(s + 1) * LANES]
    y = _gelu(jnp.concatenate([yt_ref[j] for j in range(SSM_BLOCKS)], axis=1))
    gate = jnp.dot(y.astype(BF16), gw_ref[...], preferred_element_type=F32) + gb_ref[...]
    mix_ref[:, :nssm] = (y * _sigmoid(gate)).astype(BF16)

    u0, v0 = 0, SGU_HEADS
    ssq = jnp.zeros((tm, 1), F32)
    for h in range(SGU_HEADS):
        zv = _gelu(p_ref[v0 + h].astype(F32))
        ssq = ssq + jnp.sum(zv * zv, axis=-1, keepdims=True)
    inv = lax.rsqrt(ssq / float(SGU_HEADS * LANES) + EPS)
    row = lax.broadcasted_iota(jnp.int32, (LANES, LANES), 0)
    col = lax.broadcasted_iota(jnp.int32, (LANES, LANES), 1)
    nblk = tm // LANES
    for h in range(SGU_HEADS):
        cols = slice(h * LANES, (h + 1) * LANES)
        v = (_gelu(p_ref[v0 + h].astype(F32)) * inv * ng_ref[:, cols]).astype(BF16)
        vcat = jnp.concatenate([v[r * LANES:(r + 1) * LANES, :] for r in range(nblk)], axis=1)
        w = jnp.where(col <= row, sw_ref[h], jnp.zeros((), BF16))
        mixed = jnp.dot(w, vcat, preferred_element_type=F32) + sbt_ref[:, h:h + 1]
        zu = _gelu(p_ref[u0 + h].astype(F32))
        for r in range(nblk):
            rows = slice(r * LANES, (r + 1) * LANES)
            mix_ref[rows, nssm + h * LANES:nssm + (h + 1) * LANES] = (
                zu[rows, :] * mixed[:, rows]).astype(BF16)
    m = jnp.dot(mix_ref[...], wo_ref[...], preferred_element_type=F32)
    o_ref[...] = x_ref[...] + _rms(m, g_ref[...])


def _odd_out(x, y8, p, glu_w, glu_b, sgu_g, sgu_w, sgu_bt, w_out, g, *, tm=256):
    T, D = x.shape
    tm = min(tm, T)
    nmix = w_out.shape[0]
    nssm = SSM_BLOCKS * LANES
    full2 = lambda a: pl.BlockSpec(a.shape, lambda i: (0, 0))
    return pl.pallas_call(
        functools.partial(_odd_out_kernel, tm=tm),
        grid=(T // tm,),
        in_specs=[
            pl.BlockSpec((tm, D), lambda i: (i, 0)),
            pl.BlockSpec((SSM_BLOCKS, tm // CHUNK, CHUNK * LANES), lambda i: (0, i, 0)),
            pl.BlockSpec((p.shape[0], tm, LANES), lambda i: (0, i, 0)),
            full2(glu_w), full2(glu_b), full2(sgu_g),
            pl.BlockSpec(sgu_w.shape, lambda i: (0, 0, 0)),
            full2(sgu_bt), full2(w_out), full2(g),
        ],
        out_specs=pl.BlockSpec((tm, D), lambda i: (i, 0)),
        out_shape=jax.ShapeDtypeStruct((T, D), F32),
        scratch_shapes=[pltpu.VMEM((SSM_BLOCKS, tm, LANES), F32), pltpu.VMEM((tm, nmix), BF16)],
        compiler_params=_params("parallel"),
        name="odd_out",
    )(x, y8, p, glu_w, glu_b, sgu_g, sgu_w, sgu_bt, w_out, g)


def kernel(x, norm_g, ffn_w_gate, ffn_w_up, ffn_w_down, ev_w_in, ev_pool_w, ev_pool_scale, ev_w_out, od_w_in, od_ssm_a_re, od_ssm_a_im, od_ssm_log_dt, od_ssm_b_re, od_ssm_b_im, od_ssm_c_re, od_ssm_c_im, od_ssm_d, od_glu_w, od_glu_b, od_sgu_norm_g, od_sgu_w, od_sgu_b, od_w_out):
    B, L, D = x.shape
    depth = norm_g.shape[0]
    h = x.reshape(B * L, D)
    row = lambda a: a.reshape(1, -1)
    bf = lambda a: a.astype(BF16)
    for i in range(depth):
        g = norm_g[i]
        h = _ffn(h, row(g[0]), bf(ffn_w_gate[i, 0]), bf(ffn_w_up[i, 0]), bf(ffn_w_down[i, 0]), row(g[1]))
        j = i // 2
        if i % 2 == 0:
            p = _norm_proj(h, row(g[2]), bf(ev_w_in[j]))
            ysb = _stick_breaking(p, B, L)
            h = _even_out(h, p, ysb, bf(ev_pool_w[j]), row(ev_pool_scale[j]), bf(ev_w_out[j]), row(g[3]), L)
        else:
            u2, p = _norm_proj(h, row(g[2]), bf(od_w_in[j]), chunked=True)
            prep = _ssm_prep(od_ssm_a_re[j], od_ssm_a_im[j], od_ssm_log_dt[j], od_ssm_b_re[j],
                             od_ssm_b_im[j], od_ssm_c_re[j], od_ssm_c_im[j])
            wcat, qcat, kblk, lam = _ssm_block_weights(*prep)
            dsk = jnp.tile(od_ssm_d[j].reshape(SSM_BLOCKS, 1, LANES), (1, 1, CHUNK))
            y8 = _ssm(u2, wcat, qcat, kblk, lam, dsk, B)
            h = _odd_out(h, y8, p, bf(od_glu_w[j]), row(od_glu_b[j]), row(od_sgu_norm_g[j]),
                         bf(od_sgu_w[j]), jnp.transpose(od_sgu_b[j]), bf(od_w_out[j]), row(g[3]))
        h = _ffn(h, row(g[4]), bf(ffn_w_gate[i, 1]), bf(ffn_w_up[i, 1]), bf(ffn_w_down[i, 1]), row(g[5]))
    return h.reshape(B, L, D)
```

```python
import functools
import math

import jax
import jax.numpy as jnp
from jax import lax
from jax.experimental import pallas as pl
from jax.experimental.pallas import tpu as pltpu

F32 = jnp.float32
BF16 = jnp.bfloat16

LANES = 128
EPS = 1e-6
POOL_WINDOWS = (2, 4, 8, 16)
POOL_HALO = 16
N_POOL_BLOCKS = 4
SB_HEADS = 12
SGU_HEADS = 8
SSM_BLOCKS = 8
SSM_GROUP = 16
SSM_STATE = 64
GROUPS_PER_BLOCK = LANES // SSM_GROUP
CHUNK = 16
VMEM_LIMIT = 56 * 1024 * 1024


def _params(*sem):
    return pltpu.CompilerParams(dimension_semantics=sem, vmem_limit_bytes=VMEM_LIMIT)


def _rms(x, g):
    return x * lax.rsqrt(jnp.mean(x * x, axis=-1, keepdims=True) + EPS) * g


def _gelu(x):
    c = math.sqrt(2.0 / math.pi)
    return x * (0.5 * (1.0 + jnp.tanh(c * (x + 0.044715 * (x * x * x)))))


def _sigmoid(x):
    return 1.0 / (1.0 + jnp.exp(-x))


def _ffn_kernel(x_ref, gin_ref, wg_ref, wu_ref, wd_ref, gout_ref, o_ref, hn_ref, *, nf):
    f = pl.program_id(1)

    @pl.when(f == 0)
    def _():
        hn_ref[...] = _rms(x_ref[...], gin_ref[...]).astype(BF16)
        o_ref[...] = jnp.zeros_like(o_ref)

    h = hn_ref[...]
    g = jnp.dot(h, wg_ref[...], preferred_element_type=F32)
    u = jnp.dot(h, wu_ref[...], preferred_element_type=F32)
    a = (g * _sigmoid(g) * u).astype(BF16)
    o_ref[...] += jnp.dot(a, wd_ref[...], preferred_element_type=F32)

    @pl.when(f == nf - 1)
    def _():
        o_ref[...] = x_ref[...] + 0.5 * _rms(o_ref[...], gout_ref[...])


def _ffn(x, g_in, wg, wu, wd, g_out, layer, half, *, tm=512, tf=512):
    T, D = x.shape
    F = wg.shape[-1]
    tm = min(tm, T)
    nf = F // tf
    return pl.pallas_call(
        functools.partial(_ffn_kernel, nf=nf),
        grid=(T // tm, nf),
        in_specs=[
            pl.BlockSpec((tm, D), lambda i, f: (i, 0)),
            pl.BlockSpec((1, D), lambda i, f: (0, 0)),
            pl.BlockSpec((None, None, D, tf), lambda i, f: (layer, half, 0, f)),
            pl.BlockSpec((None, None, D, tf), lambda i, f: (layer, half, 0, f)),
            pl.BlockSpec((None, None, tf, D), lambda i, f: (layer, half, f, 0)),
            pl.BlockSpec((1, D), lambda i, f: (0, 0)),
        ],
        out_specs=pl.BlockSpec((tm, D), lambda i, f: (i, 0)),
        out_shape=jax.ShapeDtypeStruct((T, D), F32),
        scratch_shapes=[pltpu.VMEM((tm, D), BF16)],
        compiler_params=_params("parallel", "arbitrary"),
        name="ffn",
    )(x, g_in, wg, wu, wd, g_out)


def _proj_kernel(x_ref, g_ref, w_ref, cs_ref, *rest, nb, tm, chunked):
    if chunked:
        oc_ref, ob_ref, hn_ref, res_ref = rest
    else:
        ob_ref, hn_ref = rest
    j = pl.program_id(1)

    @pl.when(j == 0)
    def _():
        hn_ref[...] = _rms(x_ref[...], g_ref[...]).astype(BF16)

    res = jnp.dot(hn_ref[...], w_ref[...], preferred_element_type=F32) * cs_ref[...]

    def store_blocked():
        for c in range(nb):
            ob_ref[c] = res[:, c * LANES:(c + 1) * LANES].astype(ob_ref.dtype)

    if not chunked:
        store_blocked()
        return

    @pl.when(j == 0)
    def _():
        for c in range(nb):
            res_ref[c] = res[:, c * LANES:(c + 1) * LANES]
        for c in range(nb):
            for s in range(CHUNK):
                piece = res_ref[c, pl.ds(s, tm // CHUNK, stride=CHUNK), :]
                oc_ref[c, :, s * LANES:(s + 1) * LANES] = piece.astype(oc_ref.dtype)

    pl.when(j > 0)(store_blocked)


def _norm_proj(x, g, w, col_scale, *, chunked=False, tm=1024, tn=1024):
    T, D = x.shape
    N = w.shape[1]
    tm = min(tm, T)
    nb = tn // LANES
    nj = N // tn
    in_specs = [
        pl.BlockSpec((tm, D), lambda i, j: (i, 0)),
        pl.BlockSpec((1, D), lambda i, j: (0, 0)),
        pl.BlockSpec((D, tn), lambda i, j: (0, j)),
        pl.BlockSpec((1, tn), lambda i, j: (0, j)),
    ]
    scratch = [pltpu.VMEM((tm, D), BF16)]
    if chunked:
        out_specs = (pl.BlockSpec((nb, tm // CHUNK, CHUNK * LANES), lambda i, j: (0, i, 0)),
                     pl.BlockSpec((nb, tm, LANES), lambda i, j: (jnp.maximum(j - 1, 0), i, 0)))
        out_shape = (jax.ShapeDtypeStruct((nb, T // CHUNK, CHUNK * LANES), BF16),
                     jax.ShapeDtypeStruct(((nj - 1) * nb, T, LANES), BF16))
        scratch.append(pltpu.VMEM((nb, tm, LANES), F32))
    else:
        out_specs = pl.BlockSpec((nb, tm, LANES), lambda i, j: (j, i, 0))
        out_shape = jax.ShapeDtypeStruct((nj * nb, T, LANES), BF16)
    return pl.pallas_call(
        functools.partial(_proj_kernel, nb=nb, tm=tm, chunked=chunked),
        grid=(T // tm, nj),
        in_specs=in_specs,
        out_specs=out_specs,
        out_shape=out_shape,
        scratch_shapes=scratch,
        compiler_params=_params("parallel", "arbitrary"),
        name="norm_proj_chunked" if chunked else "norm_proj",
    )(x, g, w, col_scale)


def _sb_kernel(q_ref, k_ref, v_ref, o_ref, run_ref, acc_ref, *, tq, tk, hp):
    qi = pl.program_id(2)
    nsub = tq // tk
    krow = lax.broadcasted_iota(jnp.int32, (tk, tk), 0)
    kcol = lax.broadcasted_iota(jnp.int32, (tk, tk), 1)
    suffix = (krow >= kcol).astype(BF16)
    run_ref[...] = jnp.zeros_like(run_ref)
    acc_ref[...] = jnp.zeros_like(acc_ref)

    def tile(h, j, r0):
        rows = slice(0 if r0 is None else r0, tq)
        m = tq - rows.start
        start = pl.multiple_of(j * tk, tk)
        kj = k_ref[h, pl.ds(start, tk), :]
        vj = v_ref[h, pl.ds(start, tk), :]
        z = lax.dot_general(q_ref[h, rows, :], kj, (((1,), (1,)), ((), ())),
                            preferred_element_type=F32)
        used = jnp.maximum(z, 0.0) + jnp.log(1.0 + jnp.exp(-jnp.abs(z)))
        if r0 is not None:
            causal = (lax.broadcasted_iota(jnp.int32, (m, tk), 1)
                      < lax.broadcasted_iota(jnp.int32, (m, tk), 0))
            used = jnp.where(causal, used, 0.0)
        hi = used.astype(BF16)
        lo = (used - hi.astype(F32)).astype(BF16)
        both = jnp.dot(jnp.concatenate([hi, lo], axis=0), suffix, preferred_element_type=F32)
        w = jnp.exp(z - (both[:m] + both[m:]) - run_ref[h, rows, :])
        if r0 is not None:
            w = jnp.where(causal, w, 0.0)
        acc_ref[h, rows, :] += jnp.dot(w.astype(BF16), vj, preferred_element_type=F32)
        run_ref[h, rows, :] += jnp.sum(used, axis=1, keepdims=True)

    for sub in reversed(range(nsub)):
        for h in range(hp):
            tile(h, qi * nsub + sub, sub * tk)

    def below(i, c):
        for h in range(hp):
            tile(h, qi * nsub - 1 - i, None)
        return c

    lax.fori_loop(0, qi * nsub, below, 0)
    for h in range(hp):
        o_ref[h] = acc_ref[h].astype(o_ref.dtype)


def _stick_breaking(p, B, L, *, tq=4096, tk=256, hp=1):
    T = p.shape[1]
    tq = min(tq, L)
    tk = min(tk, tq)
    nq = L // tq
    q0, k0, v0 = (blk // hp for blk in
                  (N_POOL_BLOCKS, N_POOL_BLOCKS + SB_HEADS, N_POOL_BLOCKS + 2 * SB_HEADS))
    return pl.pallas_call(
        functools.partial(_sb_kernel, tq=tq, tk=tk, hp=hp),
        grid=(B, SB_HEADS // hp, nq),
        in_specs=[
            pl.BlockSpec((hp, tq, LANES), lambda b, h, i: (q0 + h, b * nq + i, 0)),
            pl.BlockSpec((hp, L, LANES), lambda b, h, i: (k0 + h, b, 0)),
            pl.BlockSpec((hp, L, LANES), lambda b, h, i: (v0 + h, b, 0)),
        ],
        out_specs=pl.BlockSpec((hp, tq, LANES), lambda b, h, i: (h, b * nq + i, 0)),
        out_shape=jax.ShapeDtypeStruct((SB_HEADS, T, LANES), BF16),
        scratch_shapes=[pltpu.VMEM((hp, tq, 1), F32), pltpu.VMEM((hp, tq, LANES), F32)],
        compiler_params=_params("parallel", "parallel", "arbitrary"),
        name="stick_breaking",
    )(p, p, p)


def _even_out_kernel(x_ref, up_ref, halo_ref, ysb_ref, pw_ref, ps_ref, wo_ref, g_ref, o_ref,
                     uext_ref, mix_ref, *, tm, L):
    t0 = (pl.program_id(0) * tm) % L
    tpos = t0 + lax.broadcasted_iota(jnp.int32, (tm, 1), 0)
    for gi, win in enumerate(POOL_WINDOWS):
        u = up_ref[gi].astype(F32)
        halo = halo_ref[gi].astype(F32)
        uext_ref[0:POOL_HALO, :] = jnp.where(t0 == 0, 0.0, halo)
        uext_ref[POOL_HALO:, :] = u
        s = u
        for d in range(1, win):
            s = s + uext_ref[POOL_HALO - d:POOL_HALO - d + tm, :]
        count = jnp.minimum((tpos + 1).astype(F32), float(win))
        pooled = s / count - u
        y = jnp.dot(pooled.astype(BF16), pw_ref[gi], preferred_element_type=F32)
        y = y * ps_ref[:, gi * LANES:(gi + 1) * LANES]
        mix_ref[:, gi * LANES:(gi + 1) * LANES] = y.astype(BF16)
    for h in range(SB_HEADS):
        c0 = (N_POOL_BLOCKS + h) * LANES
        mix_ref[:, c0:c0 + LANES] = ysb_ref[h]
    m = jnp.dot(mix_ref[...], wo_ref[...], preferred_element_type=F32)
    o_ref[...] = x_ref[...] + _rms(m, g_ref[...])


def _even_out(x, p, ysb, pool_w, pool_scale, w_out, g, L, *, tm=512):
    T, D = x.shape
    tm = min(tm, L)
    hb = tm // POOL_HALO
    nmix = w_out.shape[0]
    return pl.pallas_call(
        functools.partial(_even_out_kernel, tm=tm, L=L),
        grid=(T // tm,),
        in_specs=[
            pl.BlockSpec((tm, D), lambda i: (i, 0)),
            pl.BlockSpec((N_POOL_BLOCKS, tm, LANES), lambda i: (0, i, 0)),
            pl.BlockSpec((N_POOL_BLOCKS, POOL_HALO, LANES), lambda i: (0, jnp.maximum(i * hb - 1, 0), 0)),
            pl.BlockSpec((SB_HEADS, tm, LANES), lambda i: (0, i, 0)),
            pl.BlockSpec(pool_w.shape, lambda i: (0, 0, 0)),
            pl.BlockSpec((1, N_POOL_BLOCKS * LANES), lambda i: (0, 0)),
            pl.BlockSpec((nmix, D), lambda i: (0, 0)),
            pl.BlockSpec((1, D), lambda i: (0, 0)),
        ],
        out_specs=pl.BlockSpec((tm, D), lambda i: (i, 0)),
        out_shape=jax.ShapeDtypeStruct((T, D), F32),
        scratch_shapes=[pltpu.VMEM((tm + POOL_HALO, LANES), F32), pltpu.VMEM((tm, nmix), BF16)],
        compiler_params=_params("parallel"),
        name="even_out",
    )(x, p, p, ysb, pool_w, pool_scale, w_out, g)


def _ssm_prep_kernel(are_ref, aim_ref, ldt_ref, bre_ref, bim_ref, cre_ref, cim_ref,
                     wc_ref, qc_ref, kd_ref, lre_ref, lim_ref):
    dt = jnp.exp(ldt_ref[...])
    are, aim = are_ref[...], aim_ref[...]

    def lam_pow(k):
        mag = jnp.exp(float(k) * are * dt)
        ang = float(k) * aim * dt
        return mag * jnp.cos(ang), mag * jnp.sin(ang)

    pows = [lam_pow(k) for k in range(CHUNK + 1)]
    lam_re, lam_im = pows[1]
    den = are * are + aim * aim
    nr = lam_re - 1.0
    f_re = (nr * are + lam_im * aim) / den
    f_im = (lam_im * are - nr * aim) / den
    b_re, b_im = bre_ref[...], bim_ref[...]
    bb_re = f_re * b_re - f_im * b_im
    bb_im = f_re * b_im + f_im * b_re
    c_re, c_im = cre_ref[...], cim_ref[...]
    lre_ref[...], lim_ref[...] = pows[CHUNK]

    for s in range(CHUNK):
        pr, pi = pows[CHUNK - 1 - s]
        wc_ref[s] = jnp.concatenate([pr * bb_re - pi * bb_im, pr * bb_im + pi * bb_re],
                                    axis=-1).astype(wc_ref.dtype)
    for t in range(CHUNK):
        pr, pi = pows[t + 1]
        qc_ref[t] = jnp.concatenate([c_re * pr - c_im * pi, -(c_re * pi + c_im * pr)],
                                    axis=-1).astype(qc_ref.dtype)
    lags = []
    for d in range(CHUNK):
        pr, pi = pows[d]
        cl_re = c_re * pr - c_im * pi
        cl_im = c_re * pi + c_im * pr
        lags.append(jnp.einsum('gcn,gxn->gcx', bb_re, cl_re, preferred_element_type=F32,
                               precision=lax.Precision.HIGHEST)
                    - jnp.einsum('gcn,gxn->gcx', bb_im, cl_im, preferred_element_type=F32,
                                 precision=lax.Precision.HIGHEST))
    kd_ref[...] = jnp.concatenate(lags, axis=-1).astype(kd_ref.dtype)


def _ssm_prep(a_re, a_im, log_dt, b_re, b_im, c_re, c_im):
    G, N = a_re.shape
    C = b_re.shape[2]
    P = GROUPS_PER_BLOCK
    J = G // P
    v3 = lambda a: a.reshape(G, 1, -1)
    bt = lambda a: jnp.transpose(a, (0, 2, 1))
    spec3 = lambda s1, s2: pl.BlockSpec((P, s1, s2), lambda i: (i, 0, 0))
    spec5 = pl.BlockSpec((None, CHUNK, P, C, 2 * N), lambda i: (i, 0, 0, 0, 0))
    wc, qc, kd, l_re, l_im = pl.pallas_call(
        _ssm_prep_kernel,
        grid=(J,),
        in_specs=[spec3(1, N), spec3(1, N), spec3(1, 1), spec3(C, N), spec3(C, N), spec3(C, N), spec3(C, N)],
        out_specs=(spec5, spec5, spec3(C, CHUNK * C), spec3(1, N), spec3(1, N)),
        out_shape=(
            jax.ShapeDtypeStruct((J, CHUNK, P, C, 2 * N), BF16),
            jax.ShapeDtypeStruct((J, CHUNK, P, C, 2 * N), BF16),
            jax.ShapeDtypeStruct((G, C, CHUNK * C), BF16),
            jax.ShapeDtypeStruct((G, 1, N), F32),
            jax.ShapeDtypeStruct((G, 1, N), F32),
        ),
        compiler_params=_params("parallel"),
        name="ssm_prep",
    )(v3(a_re), v3(a_im), log_dt.reshape(G, 1, 1), bt(b_re), bt(b_im), c_re, c_im)
    lam = jnp.concatenate([l_re.reshape(J, 1, P * N), l_im.reshape(J, 1, P * N)], axis=-1)
    rows = CHUNK * P * C
    return wc.reshape(J, rows, 2 * N), qc.reshape(J, rows, 2 * N), kd.reshape(J, P * C, CHUNK * C), lam


def _spread_to_block_diagonal(compact_ref, out_ref):
    half = SSM_STATE
    wide = 2 * GROUPS_PER_BLOCK * half
    k = lax.broadcasted_iota(jnp.int32, (2 * half, wide), 0)
    c = lax.broadcasted_iota(jnp.int32, (2 * half, wide), 1)
    spread = ((k // half == c // (GROUPS_PER_BLOCK * half)) & (k % half == c % half)).astype(BF16)
    r = lax.broadcasted_iota(jnp.int32, (LANES, wide), 0)
    c = lax.broadcasted_iota(jnp.int32, (LANES, wide), 1)
    own = r // SSM_GROUP == (c // half) % GROUPS_PER_BLOCK
    for s in range(compact_ref.shape[0] // LANES):
        rows = slice(s * LANES, (s + 1) * LANES)
        full = jnp.dot(compact_ref[rows, :], spread, preferred_element_type=F32)
        out_ref[rows, :] = jnp.where(own, full, 0.0).astype(out_ref.dtype)


def _ssm_state_kernel(u_ref, wc_ref, lam_ref, xs_ref, w_ref, v_ref, xf_ref, *, B, nch, rt):
    R = B * nch
    half = GROUPS_PER_BLOCK * SSM_STATE
    _spread_to_block_diagonal(wc_ref, w_ref)
    for r in range(R // rt):
        v_ref[r * rt:(r + 1) * rt, :] = jnp.dot(u_ref[r * rt:(r + 1) * rt, :], w_ref[...],
                                                preferred_element_type=F32)
    lr, li = lam_ref[:, :half], lam_ref[:, half:]

    def body(n, carry):
        out = []
        for b in range(B):
            xr, xi = carry[2 * b], carry[2 * b + 1]
            row = b * nch + n
            xf_ref[pl.ds(row, 1), :half] = xr
            xf_ref[pl.ds(row, 1), half:] = xi
            v = v_ref[pl.ds(row, 1), :]
            out.append(lr * xr - li * xi + v[:, :half])
            out.append(lr * xi + li * xr + v[:, half:])
        return tuple(out)

    lax.fori_loop(0, nch, body, tuple(jnp.zeros((1, half), F32) for _ in range(2 * B)))
    xs_ref[...] = xf_ref[...].astype(BF16)


def _ssm_out_kernel(u_ref, xs_ref, kd_ref, qc_ref, dsk_ref, y_ref, toep_ref, qt_ref):
    @pl.when((pl.program_id(0) == 0) & (pl.program_id(1) == 0))
    def _():
        toep_ref[...] = jnp.zeros_like(toep_ref)

    @pl.when(pl.program_id(1) == 0)
    def _():
        _spread_to_block_diagonal(qc_ref, qt_ref)
        nk, wide = CHUNK * SSM_GROUP, CHUNK * LANES
        k = lax.broadcasted_iota(jnp.int32, (nk, wide), 0)
        c = lax.broadcasted_iota(jnp.int32, (nk, wide), 1)
        spread = ((k // SSM_GROUP == c // LANES) & (k % SSM_GROUP == c % SSM_GROUP)).astype(BF16)
        r = lax.broadcasted_iota(jnp.int32, (LANES, wide), 0)
        c = lax.broadcasted_iota(jnp.int32, (LANES, wide), 1)
        own = r // SSM_GROUP == (c // SSM_GROUP) % GROUPS_PER_BLOCK
        lag = jnp.where(own, jnp.dot(kd_ref[...], spread, preferred_element_type=F32), 0.0).astype(BF16)
        for s in range(CHUNK):
            for t in range(s, CHUNK):
                toep_ref[s * LANES:(s + 1) * LANES, t * LANES:(t + 1) * LANES] = (
                    lag[:, (t - s) * LANES:(t - s + 1) * LANES])

    u = u_ref[...]
    y_ref[...] = (jnp.dot(u, toep_ref[...], preferred_element_type=F32)
                  + lax.dot_general(xs_ref[...], qt_ref[...], (((1,), (1,)), ((), ())),
                                    preferred_element_type=F32)
                  + dsk_ref[...] * u.astype(F32))


def _ssm(u2, wc, qc, kd, lam, dsk, B):
    R = u2.shape[1]
    nch = R // B
    rt = min(256, R)
    wide = CHUNK * LANES
    nstate = 2 * GROUPS_PER_BLOCK * SSM_STATE
    xs = pl.pallas_call(
        functools.partial(_ssm_state_kernel, B=B, nch=nch, rt=rt),
        grid=(SSM_BLOCKS,),
        in_specs=[
            pl.BlockSpec((None, R, wide), lambda j: (j, 0, 0)),
            pl.BlockSpec((None, wide, 2 * SSM_STATE), lambda j: (j, 0, 0)),
            pl.BlockSpec((None, 1, nstate), lambda j: (j, 0, 0)),
        ],
        out_specs=pl.BlockSpec((None, R, nstate), lambda j: (j, 0, 0)),
        out_shape=jax.ShapeDtypeStruct((SSM_BLOCKS, R, nstate), BF16),
        scratch_shapes=[pltpu.VMEM((wide, nstate), BF16), pltpu.VMEM((R, nstate), F32),
                        pltpu.VMEM((R, nstate), F32)],
        compiler_params=_params("parallel"),
        name="ssm_state",
    )(u2, wc, lam)
    return pl.pallas_call(
        _ssm_out_kernel,
        grid=(SSM_BLOCKS, R // rt),
        in_specs=[
            pl.BlockSpec((None, rt, wide), lambda j, r: (j, r, 0)),
            pl.BlockSpec((None, rt, nstate), lambda j, r: (j, r, 0)),
            pl.BlockSpec((None, LANES, CHUNK * SSM_GROUP), lambda j, r: (j, 0, 0)),
            pl.BlockSpec((None, wide, 2 * SSM_STATE), lambda j, r: (j, 0, 0)),
            pl.BlockSpec((None, 1, wide), lambda j, r: (j, 0, 0)),
        ],
        out_specs=pl.BlockSpec((None, rt, wide), lambda j, r: (j, r, 0)),
        out_shape=jax.ShapeDtypeStruct((SSM_BLOCKS, R, wide), F32),
        scratch_shapes=[pltpu.VMEM((wide, wide), BF16), pltpu.VMEM((wide, nstate), BF16)],
        compiler_params=_params("arbitrary", "arbitrary"),
        name="ssm_out",
    )(u2, xs, kd, qc, dsk)


def _odd_out_kernel(x_ref, y_ref, p_ref, gw_ref, gb_ref, ng_ref, sw_ref, sbt_ref, wo_ref,
                    g_ref, o_ref, yt_ref, mix_ref, *, tm):
    nssm = SSM_BLOCKS * LANES
    for j in range(SSM_BLOCKS):
        for s in range(CHUNK):
            yt_ref[j, pl.ds(s, tm // CHUNK, stride=CHUNK), :] = y_ref[j, :, s * LANES:(s + 1) * LANES]
    y = _gelu(jnp.concatenate([yt_ref[j] for j in range(SSM_BLOCKS)], axis=1))
    gate = jnp.dot(y.astype(BF16), gw_ref[...], preferred_element_type=F32) + gb_ref[...]
    mix_ref[:, :nssm] = (y * _sigmoid(gate)).astype(BF16)

    u0, v0 = 0, SGU_HEADS
    ssq = jnp.zeros((tm, 1), F32)
    for h in range(SGU_HEADS):
        zv = _gelu(p_ref[v0 + h].astype(F32))
        ssq = ssq + jnp.sum(zv * zv, axis=-1, keepdims=True)
    inv = lax.rsqrt(ssq / float(SGU_HEADS * LANES) + EPS)
    row = lax.broadcasted_iota(jnp.int32, (LANES, LANES), 0)
    col = lax.broadcasted_iota(jnp.int32, (LANES, LANES), 1)
    nblk = tm // LANES
    for h in range(SGU_HEADS):
        cols = slice(h * LANES, (h + 1) * LANES)
        v = (_gelu(p_ref[v0 + h].astype(F32)) * inv * ng_ref[:, cols]).astype(BF16)
        vcat = jnp.concatenate([v[r * LANES:(r + 1) * LANES, :] for r in range(nblk)], axis=1)
        w = jnp.where(col <= row, sw_ref[h], jnp.zeros((), BF16))
        mixed = jnp.dot(w, vcat, preferred_element_type=F32) + sbt_ref[:, h:h + 1]
        zu = _gelu(p_ref[u0 + h].astype(F32))
        for r in range(nblk):
            rows = slice(r * LANES, (r + 1) * LANES)
            mix_ref[rows, nssm + h * LANES:nssm + (h + 1) * LANES] = (
                zu[rows, :] * mixed[:, rows]).astype(BF16)
    m = jnp.dot(mix_ref[...], wo_ref[...], preferred_element_type=F32)
    o_ref[...] = x_ref[...] + _rms(m, g_ref[...])


def _odd_out(x, y8, p, glu_w, glu_b, sgu_g, sgu_w, sgu_bt, w_out, g, *, tm=256):
    T, D = x.shape
    tm = min(tm, T)
    nmix = w_out.shape[0]
    nssm = SSM_BLOCKS * LANES
    full2 = lambda a: pl.BlockSpec(a.shape, lambda i: (0, 0))
    return pl.pallas_call(
        functools.partial(_odd_out_kernel, tm=tm),
        grid=(T // tm,),
        in_specs=[
            pl.BlockSpec((tm, D), lambda i: (i, 0)),
            pl.BlockSpec((SSM_BLOCKS, tm // CHUNK, CHUNK * LANES), lambda i: (0, i, 0)),
            pl.BlockSpec((p.shape[0], tm, LANES), lambda i: (0, i, 0)),
            full2(glu_w), full2(glu_b), full2(sgu_g),
            pl.BlockSpec(sgu_w.shape, lambda i: (0, 0, 0)),
            full2(sgu_bt), full2(w_out), full2(g),
        ],
        out_specs=pl.BlockSpec((tm, D), lambda i: (i, 0)),
        out_shape=jax.ShapeDtypeStruct((T, D), F32),
        scratch_shapes=[pltpu.VMEM((SSM_BLOCKS, tm, LANES), F32), pltpu.VMEM((tm, nmix), BF16)],
        compiler_params=_params("parallel"),
        name="odd_out",
    )(x, y8, p, glu_w, glu_b, sgu_g, sgu_w, sgu_bt, w_out, g)


def kernel(x, norm_g, ffn_w_gate, ffn_w_up, ffn_w_down, ev_w_in, ev_pool_w, ev_pool_scale, ev_w_out, od_w_in, od_ssm_a_re, od_ssm_a_im, od_ssm_log_dt, od_ssm_b_re, od_ssm_b_im, od_ssm_c_re, od_ssm_c_im, od_ssm_d, od_glu_w, od_glu_b, od_sgu_norm_g, od_sgu_w, od_sgu_b, od_w_out):
    B, L, D = x.shape
    depth = norm_g.shape[0]
    h = x.reshape(B * L, D)
    row = lambda a: a.reshape(1, -1)
    bf = lambda a: a.astype(BF16)
    wg, wu, wd = bf(ffn_w_gate), bf(ffn_w_up), bf(ffn_w_down)
    for i in range(depth):
        g = norm_g[i]
        h = _ffn(h, row(g[0]), wg, wu, wd, row(g[1]), i, 0)
        j = i // 2
        if i % 2 == 0:
            w_in = ev_w_in[j]
            q_cols = (jnp.arange(w_in.shape[1]) // LANES >= N_POOL_BLOCKS) & (
                jnp.arange(w_in.shape[1]) // LANES < N_POOL_BLOCKS + SB_HEADS)
            col_scale = jnp.where(q_cols, LANES ** -0.5, 1.0).astype(F32).reshape(1, -1)
            p = _norm_proj(h, row(g[2]), bf(w_in), col_scale)
            ysb = _stick_breaking(p, B, L)
            h = _even_out(h, p, ysb, bf(ev_pool_w[j]), row(ev_pool_scale[j]), bf(ev_w_out[j]), row(g[3]), L)
        else:
            w_in = od_w_in[j]
            u2, p = _norm_proj(h, row(g[2]), bf(w_in), jnp.ones((1, w_in.shape[1]), F32), chunked=True)
            wc, qc, kd, lam = _ssm_prep(od_ssm_a_re[j], od_ssm_a_im[j], od_ssm_log_dt[j], od_ssm_b_re[j],
                                        od_ssm_b_im[j], od_ssm_c_re[j], od_ssm_c_im[j])
            dsk = jnp.tile(od_ssm_d[j].reshape(SSM_BLOCKS, 1, LANES), (1, 1, CHUNK))
            y8 = _ssm(u2, wc, qc, kd, lam, dsk, B)
            h = _odd_out(h, y8, p, bf(od_glu_w[j]), row(od_glu_b[j]), row(od_sgu_norm_g[j]),
                         bf(od_sgu_w[j]), jnp.transpose(od_sgu_b[j]), bf(od_w_out[j]), row(g[3]))
        h = _ffn(h, row(g[4]), wg, wu, wd, row(g[5]), i, 1)
    return h.reshape(B, L, D)
```

```python
import functools
import math

import jax
import jax.numpy as jnp
from jax import lax
from jax.experimental import pallas as pl
from jax.experimental.pallas import tpu as pltpu

F32 = jnp.float32
BF16 = jnp.bfloat16

LANES = 128
EPS = 1e-6
POOL_WINDOWS = (2, 4, 8, 16)
POOL_HALO = 16
N_POOL_BLOCKS = 4
SB_HEADS = 12
SGU_HEADS = 8
SSM_BLOCKS = 8
SSM_GROUP = 16
SSM_STATE = 64
GROUPS_PER_BLOCK = LANES // SSM_GROUP
CHUNK = 16
VMEM_LIMIT = 56 * 1024 * 1024
FFN_VMEM_LIMIT = 60 * 1024 * 1024
FFN_SLICE = 48


def _params(*sem):
    return pltpu.CompilerParams(dimension_semantics=sem, vmem_limit_bytes=VMEM_LIMIT)


def _rms(x, g):
    return x * lax.rsqrt(jnp.mean(x * x, axis=-1, keepdims=True) + EPS) * g


def _gelu(x):
    c = math.sqrt(2.0 / math.pi)
    return x * (0.5 * (1.0 + jnp.tanh(c * (x + 0.044715 * (x * x * x)))))


def _sigmoid(x):
    return 1.0 / (1.0 + jnp.exp(-x))


def _ffn_kernel(xp_ref, xn_ref, gin_ref, wg_ref, wu_ref, wd_ref, gout_ref, o_ref, hn_ref, acc_ref,
                *, nt, nf, tm):
    i = pl.program_id(0)
    f = pl.program_id(1)
    cur = i % 2
    oth = 1 - cur

    @pl.when((i == 0) & (f == 0))
    def _():
        hn_ref[0] = _rms(xp_ref[...], gin_ref[...]).astype(BF16)
        acc_ref[...] = jnp.zeros_like(acc_ref)

    rs = pl.ds(pl.multiple_of(jnp.minimum(f * FFN_SLICE, tm - FFN_SLICE), 16), FFN_SLICE)

    def neighbours():
        o_ref[rs, :] = xp_ref[rs, :] + 0.5 * _rms(acc_ref[oth, rs, :], gout_ref[...])
        hn_ref[oth, rs, :] = _rms(xn_ref[rs, :], gin_ref[...]).astype(BF16)

    @pl.when(i < nt)
    def _():
        neighbours()
        h = hn_ref[cur]
        g = jnp.dot(h, wg_ref[...], preferred_element_type=F32)
        u = jnp.dot(h, wu_ref[...], preferred_element_type=F32)
        a = (g * _sigmoid(g) * u).astype(BF16)
        start = jnp.where(f == 0, 0.0, acc_ref[cur])
        acc_ref[cur] = start + jnp.dot(a, wd_ref[...], preferred_element_type=F32)

    pl.when(i == nt)(neighbours)


def _ffn(x, g_in, wg, wu, wd, g_out, layer, half, *, tm=512, tf=512):
    T, D = x.shape
    F = wg.shape[-1]
    tm = min(tm, T)
    nt = T // tm
    nf = F // tf
    assert nf * FFN_SLICE >= tm >= FFN_SLICE, 'the per-step row slices must cover a tile'
    chunk = lambda i, f: jnp.where(i == nt, nf - 1, f)
    return pl.pallas_call(
        functools.partial(_ffn_kernel, nt=nt, nf=nf, tm=tm),
        grid=(nt + 1, nf),
        in_specs=[
            pl.BlockSpec((tm, D), lambda i, f: (jnp.maximum(i - 1, 0), 0)),
            pl.BlockSpec((tm, D), lambda i, f: (jnp.minimum(i + 1, nt - 1), 0)),
            pl.BlockSpec((1, D), lambda i, f: (0, 0)),
            pl.BlockSpec((None, None, D, tf), lambda i, f: (layer, half, 0, chunk(i, f))),
            pl.BlockSpec((None, None, D, tf), lambda i, f: (layer, half, 0, chunk(i, f))),
            pl.BlockSpec((None, None, tf, D), lambda i, f: (layer, half, chunk(i, f), 0)),
            pl.BlockSpec((1, D), lambda i, f: (0, 0)),
        ],
        out_specs=pl.BlockSpec((tm, D), lambda i, f: (jnp.maximum(i - 1, 0), 0)),
        out_shape=jax.ShapeDtypeStruct((T, D), F32),
        scratch_shapes=[pltpu.VMEM((2, tm, D), BF16), pltpu.VMEM((2, tm, D), F32)],
        compiler_params=pltpu.CompilerParams(dimension_semantics=("arbitrary", "arbitrary"),
                                             vmem_limit_bytes=FFN_VMEM_LIMIT),
        name="ffn",
    )(x, x, g_in, wg, wu, wd, g_out)


def _proj_kernel(x_ref, g_ref, w_ref, cs_ref, *rest, nb, tm, chunked):
    if chunked:
        oc_ref, ob_ref, hn_ref, res_ref = rest
    else:
        ob_ref, hn_ref = rest
    j = pl.program_id(1)

    @pl.when(j == 0)
    def _():
        hn_ref[...] = _rms(x_ref[...], g_ref[...]).astype(BF16)

    res = jnp.dot(hn_ref[...], w_ref[...], preferred_element_type=F32) * cs_ref[...]

    def store_blocked():
        for c in range(nb):
            ob_ref[c] = res[:, c * LANES:(c + 1) * LANES].astype(ob_ref.dtype)

    if not chunked:
        store_blocked()
        return

    @pl.when(j == 0)
    def _():
        for c in range(nb):
            res_ref[c] = res[:, c * LANES:(c + 1) * LANES]
        for c in range(nb):
            for s in range(CHUNK):
                piece = res_ref[c, pl.ds(s, tm // CHUNK, stride=CHUNK), :]
                oc_ref[c, :, s * LANES:(s + 1) * LANES] = piece.astype(oc_ref.dtype)

    pl.when(j > 0)(store_blocked)


def _norm_proj(x, g, w, col_scale, *, chunked=False, tm=1024, tn=1024):
    T, D = x.shape
    N = w.shape[1]
    tm = min(tm, T)
    nb = tn // LANES
    nj = N // tn
    in_specs = [
        pl.BlockSpec((tm, D), lambda i, j: (i, 0)),
        pl.BlockSpec((1, D), lambda i, j: (0, 0)),
        pl.BlockSpec((D, tn), lambda i, j: (0, j)),
        pl.BlockSpec((1, tn), lambda i, j: (0, j)),
    ]
    scratch = [pltpu.VMEM((tm, D), BF16)]
    if chunked:
        out_specs = (pl.BlockSpec((nb, tm // CHUNK, CHUNK * LANES), lambda i, j: (0, i, 0)),
                     pl.BlockSpec((nb, tm, LANES), lambda i, j: (jnp.maximum(j - 1, 0), i, 0)))
        out_shape = (jax.ShapeDtypeStruct((nb, T // CHUNK, CHUNK * LANES), BF16),
                     jax.ShapeDtypeStruct(((nj - 1) * nb, T, LANES), BF16))
        scratch.append(pltpu.VMEM((nb, tm, LANES), F32))
    else:
        out_specs = pl.BlockSpec((nb, tm, LANES), lambda i, j: (j, i, 0))
        out_shape = jax.ShapeDtypeStruct((nj * nb, T, LANES), BF16)
    return pl.pallas_call(
        functools.partial(_proj_kernel, nb=nb, tm=tm, chunked=chunked),
        grid=(T // tm, nj),
        in_specs=in_specs,
        out_specs=out_specs,
        out_shape=out_shape,
        scratch_shapes=scratch,
        compiler_params=_params("parallel", "arbitrary"),
        name="norm_proj_chunked" if chunked else "norm_proj",
    )(x, g, w, col_scale)


def _sb_kernel(q_ref, k_ref, v_ref, o_ref, run_ref, acc_ref, *, tq, tk, hp):
    qi = pl.program_id(2)
    nsub = tq // tk
    krow = lax.broadcasted_iota(jnp.int32, (tk, tk), 0)
    kcol = lax.broadcasted_iota(jnp.int32, (tk, tk), 1)
    suffix = (krow >= kcol).astype(BF16)
    run_ref[...] = jnp.zeros_like(run_ref)
    acc_ref[...] = jnp.zeros_like(acc_ref)

    def tile(h, j, r0):
        rows = slice(0 if r0 is None else r0, tq)
        m = tq - rows.start
        start = pl.multiple_of(j * tk, tk)
        kj = k_ref[h, pl.ds(start, tk), :]
        vj = v_ref[h, pl.ds(start, tk), :]
        z = lax.dot_general(q_ref[h, rows, :], kj, (((1,), (1,)), ((), ())),
                            preferred_element_type=F32)
        used = jnp.maximum(z, 0.0) + jnp.log(1.0 + jnp.exp(-jnp.abs(z)))
        if r0 is not None:
            causal = (lax.broadcasted_iota(jnp.int32, (m, tk), 1)
                      < lax.broadcasted_iota(jnp.int32, (m, tk), 0))
            used = jnp.where(causal, used, 0.0)
        hi = used.astype(BF16)
        lo = (used - hi.astype(F32)).astype(BF16)
        both = jnp.dot(jnp.concatenate([hi, lo], axis=0), suffix, preferred_element_type=F32)
        w = jnp.exp(z - (both[:m] + both[m:]) - run_ref[h, rows, :])
        if r0 is not None:
            w = jnp.where(causal, w, 0.0)
        acc_ref[h, rows, :] += jnp.dot(w.astype(BF16), vj, preferred_element_type=F32)
        run_ref[h, rows, :] += jnp.sum(used, axis=1, keepdims=True)

    for sub in reversed(range(nsub)):
        for h in range(hp):
            tile(h, qi * nsub + sub, sub * tk)

    def below(i, c):
        for h in range(hp):
            tile(h, qi * nsub - 1 - i, None)
        return c

    lax.fori_loop(0, qi * nsub, below, 0)
    for h in range(hp):
        o_ref[h] = acc_ref[h].astype(o_ref.dtype)


def _stick_breaking(p, B, L, *, tq=4096, tk=256, hp=1):
    T = p.shape[1]
    tq = min(tq, L)
    tk = min(tk, tq)
    nq = L // tq
    q0, k0, v0 = (blk // hp for blk in
                  (N_POOL_BLOCKS, N_POOL_BLOCKS + SB_HEADS, N_POOL_BLOCKS + 2 * SB_HEADS))
    return pl.pallas_call(
        functools.partial(_sb_kernel, tq=tq, tk=tk, hp=hp),
        grid=(B, SB_HEADS // hp, nq),
        in_specs=[
            pl.BlockSpec((hp, tq, LANES), lambda b, h, i: (q0 + h, b * nq + i, 0)),
            pl.BlockSpec((hp, L, LANES), lambda b, h, i: (k0 + h, b, 0)),
            pl.BlockSpec((hp, L, LANES), lambda b, h, i: (v0 + h, b, 0)),
        ],
        out_specs=pl.BlockSpec((hp, tq, LANES), lambda b, h, i: (h, b * nq + i, 0)),
        out_shape=jax.ShapeDtypeStruct((SB_HEADS, T, LANES), BF16),
        scratch_shapes=[pltpu.VMEM((hp, tq, 1), F32), pltpu.VMEM((hp, tq, LANES), F32)],
        compiler_params=_params("parallel", "parallel", "arbitrary"),
        name="stick_breaking",
    )(p, p, p)


def _even_out_kernel(x_ref, up_ref, halo_ref, ysb_ref, pw_ref, ps_ref, wo_ref, g_ref, o_ref,
                     uext_ref, mix_ref, *, tm, L):
    t0 = (pl.program_id(0) * tm) % L
    tpos = t0 + lax.broadcasted_iota(jnp.int32, (tm, 1), 0)
    for gi, win in enumerate(POOL_WINDOWS):
        u = up_ref[gi].astype(F32)
        halo = halo_ref[gi].astype(F32)
        uext_ref[0:POOL_HALO, :] = jnp.where(t0 == 0, 0.0, halo)
        uext_ref[POOL_HALO:, :] = u
        s = u
        for d in range(1, win):
            s = s + uext_ref[POOL_HALO - d:POOL_HALO - d + tm, :]
        count = jnp.minimum((tpos + 1).astype(F32), float(win))
        pooled = s / count - u
        y = jnp.dot(pooled.astype(BF16), pw_ref[gi], preferred_element_type=F32)
        y = y * ps_ref[:, gi * LANES:(gi + 1) * LANES]
        mix_ref[:, gi * LANES:(gi + 1) * LANES] = y.astype(BF16)
    for h in range(SB_HEADS):
        c0 = (N_POOL_BLOCKS + h) * LANES
        mix_ref[:, c0:c0 + LANES] = ysb_ref[h]
    m = jnp.dot(mix_ref[...], wo_ref[...], preferred_element_type=F32)
    o_ref[...] = x_ref[...] + _rms(m, g_ref[...])


def _even_out(x, p, ysb, pool_w, pool_scale, w_out, g, L, *, tm=512):
    T, D = x.shape
    tm = min(tm, L)
    hb = tm // POOL_HALO
    nmix = w_out.shape[0]
    return pl.pallas_call(
        functools.partial(_even_out_kernel, tm=tm, L=L),
        grid=(T // tm,),
        in_specs=[
            pl.BlockSpec((tm, D), lambda i: (i, 0)),
            pl.BlockSpec((N_POOL_BLOCKS, tm, LANES), lambda i: (0, i, 0)),
            pl.BlockSpec((N_POOL_BLOCKS, POOL_HALO, LANES), lambda i: (0, jnp.maximum(i * hb - 1, 0), 0)),
            pl.BlockSpec((SB_HEADS, tm, LANES), lambda i: (0, i, 0)),
            pl.BlockSpec(pool_w.shape, lambda i: (0, 0, 0)),
            pl.BlockSpec((1, N_POOL_BLOCKS * LANES), lambda i: (0, 0)),
            pl.BlockSpec((nmix, D), lambda i: (0, 0)),
            pl.BlockSpec((1, D), lambda i: (0, 0)),
        ],
        out_specs=pl.BlockSpec((tm, D), lambda i: (i, 0)),
        out_shape=jax.ShapeDtypeStruct((T, D), F32),
        scratch_shapes=[pltpu.VMEM((tm + POOL_HALO, LANES), F32), pltpu.VMEM((tm, nmix), BF16)],
        compiler_params=_params("parallel"),
        name="even_out",
    )(x, p, p, ysb, pool_w, pool_scale, w_out, g)


def _ssm_prep_kernel(are_ref, aim_ref, ldt_ref, bre_ref, bim_ref, cre_ref, cim_ref,
                     wc_ref, qc_ref, kd_ref, lre_ref, lim_ref):
    dt = jnp.exp(ldt_ref[...])
    are, aim = are_ref[...], aim_ref[...]

    def lam_pow(k):
        mag = jnp.exp(float(k) * are * dt)
        ang = float(k) * aim * dt
        return mag * jnp.cos(ang), mag * jnp.sin(ang)

    pows = [lam_pow(k) for k in range(CHUNK + 1)]
    lam_re, lam_im = pows[1]
    den = are * are + aim * aim
    nr = lam_re - 1.0
    f_re = (nr * are + lam_im * aim) / den
    f_im = (lam_im * are - nr * aim) / den
    b_re, b_im = bre_ref[...], bim_ref[...]
    bb_re = f_re * b_re - f_im * b_im
    bb_im = f_re * b_im + f_im * b_re
    c_re, c_im = cre_ref[...], cim_ref[...]
    lre_ref[...], lim_ref[...] = pows[CHUNK]

    for s in range(CHUNK):
        pr, pi = pows[CHUNK - 1 - s]
        wc_ref[s] = jnp.concatenate([pr * bb_re - pi * bb_im, pr * bb_im + pi * bb_re],
                                    axis=-1).astype(wc_ref.dtype)
    for t in range(CHUNK):
        pr, pi = pows[t + 1]
        qc_ref[t] = jnp.concatenate([c_re * pr - c_im * pi, -(c_re * pi + c_im * pr)],
                                    axis=-1).astype(qc_ref.dtype)
    lags = []
    for d in range(CHUNK):
        pr, pi = pows[d]
        cl_re = c_re * pr - c_im * pi
        cl_im = c_re * pi + c_im * pr
        lags.append(jnp.einsum('gcn,gxn->gcx', bb_re, cl_re, preferred_element_type=F32,
                               precision=lax.Precision.HIGHEST)
                    - jnp.einsum('gcn,gxn->gcx', bb_im, cl_im, preferred_element_type=F32,
                                 precision=lax.Precision.HIGHEST))
    kd_ref[...] = jnp.concatenate(lags, axis=-1).astype(kd_ref.dtype)


def _ssm_prep(a_re, a_im, log_dt, b_re, b_im, c_re, c_im):
    G, N = a_re.shape
    C = b_re.shape[2]
    P = GROUPS_PER_BLOCK
    J = G // P
    v3 = lambda a: a.reshape(G, 1, -1)
    bt = lambda a: jnp.transpose(a, (0, 2, 1))
    spec3 = lambda s1, s2: pl.BlockSpec((P, s1, s2), lambda i: (i, 0, 0))
    spec5 = pl.BlockSpec((None, CHUNK, P, C, 2 * N), lambda i: (i, 0, 0, 0, 0))
    wc, qc, kd, l_re, l_im = pl.pallas_call(
        _ssm_prep_kernel,
        grid=(J,),
        in_specs=[spec3(1, N), spec3(1, N), spec3(1, 1), spec3(C, N), spec3(C, N), spec3(C, N), spec3(C, N)],
        out_specs=(spec5, spec5, spec3(C, CHUNK * C), spec3(1, N), spec3(1, N)),
        out_shape=(
            jax.ShapeDtypeStruct((J, CHUNK, P, C, 2 * N), BF16),
            jax.ShapeDtypeStruct((J, CHUNK, P, C, 2 * N), BF16),
            jax.ShapeDtypeStruct((G, C, CHUNK * C), BF16),
            jax.ShapeDtypeStruct((G, 1, N), F32),
            jax.ShapeDtypeStruct((G, 1, N), F32),
        ),
        compiler_params=_params("parallel"),
        name="ssm_prep",
    )(v3(a_re), v3(a_im), log_dt.reshape(G, 1, 1), bt(b_re), bt(b_im), c_re, c_im)
    lam = jnp.concatenate([l_re.reshape(J, 1, P * N), l_im.reshape(J, 1, P * N)], axis=-1)
    rows = CHUNK * P * C
    return wc.reshape(J, rows, 2 * N), qc.reshape(J, rows, 2 * N), kd.reshape(J, P * C, CHUNK * C), lam


def _spread_to_block_diagonal(compact_ref, out_ref):
    half = SSM_STATE
    wide = 2 * GROUPS_PER_BLOCK * half
    k = lax.broadcasted_iota(jnp.int32, (2 * half, wide), 0)
    c = lax.broadcasted_iota(jnp.int32, (2 * half, wide), 1)
    spread = ((k // half == c // (GROUPS_PER_BLOCK * half)) & (k % half == c % half)).astype(BF16)
    r = lax.broadcasted_iota(jnp.int32, (LANES, wide), 0)
    c = lax.broadcasted_iota(jnp.int32, (LANES, wide), 1)
    own = r // SSM_GROUP == (c // half) % GROUPS_PER_BLOCK
    for s in range(compact_ref.shape[0] // LANES):
        rows = slice(s * LANES, (s + 1) * LANES)
        full = jnp.dot(compact_ref[rows, :], spread, preferred_element_type=F32)
        out_ref[rows, :] = jnp.where(own, full, 0.0).astype(out_ref.dtype)


def _ssm_state_kernel(u_ref, wc_ref, lam_ref, xs_ref, w_ref, v_ref, xf_ref, *, B, nch, rt):
    R = B * nch
    half = GROUPS_PER_BLOCK * SSM_STATE
    _spread_to_block_diagonal(wc_ref, w_ref)
    for r in range(R // rt):
        v_ref[r * rt:(r + 1) * rt, :] = jnp.dot(u_ref[r * rt:(r + 1) * rt, :], w_ref[...],
                                                preferred_element_type=F32)
    lr, li = lam_ref[:, :half], lam_ref[:, half:]

    def body(n, carry):
        out = []
        for b in range(B):
            xr, xi = carry[2 * b], carry[2 * b + 1]
            row = b * nch + n
            xf_ref[pl.ds(row, 1), :half] = xr
            xf_ref[pl.ds(row, 1), half:] = xi
            v = v_ref[pl.ds(row, 1), :]
            out.append(lr * xr - li * xi + v[:, :half])
            out.append(lr * xi + li * xr + v[:, half:])
        return tuple(out)

    lax.fori_loop(0, nch, body, tuple(jnp.zeros((1, half), F32) for _ in range(2 * B)))
    xs_ref[...] = xf_ref[...].astype(BF16)


def _ssm_out_kernel(u_ref, xs_ref, kd_ref, qc_ref, dsk_ref, y_ref, toep_ref, qt_ref):
    @pl.when((pl.program_id(0) == 0) & (pl.program_id(1) == 0))
    def _():
        toep_ref[...] = jnp.zeros_like(toep_ref)

    @pl.when(pl.program_id(1) == 0)
    def _():
        _spread_to_block_diagonal(qc_ref, qt_ref)
        nk, wide = CHUNK * SSM_GROUP, CHUNK * LANES
        k = lax.broadcasted_iota(jnp.int32, (nk, wide), 0)
        c = lax.broadcasted_iota(jnp.int32, (nk, wide), 1)
        spread = ((k // SSM_GROUP == c // LANES) & (k % SSM_GROUP == c % SSM_GROUP)).astype(BF16)
        r = lax.broadcasted_iota(jnp.int32, (LANES, wide), 0)
        c = lax.broadcasted_iota(jnp.int32, (LANES, wide), 1)
        own = r // SSM_GROUP == (c // SSM_GROUP) % GROUPS_PER_BLOCK
        lag = jnp.where(own, jnp.dot(kd_ref[...], spread, preferred_element_type=F32), 0.0).astype(BF16)
        for s in range(CHUNK):
            for t in range(s, CHUNK):
                toep_ref[s * LANES:(s + 1) * LANES, t * LANES:(t + 1) * LANES] = (
                    lag[:, (t - s) * LANES:(t - s + 1) * LANES])

    u = u_ref[...]
    y_ref[...] = (jnp.dot(u, toep_ref[...], preferred_element_type=F32)
                  + lax.dot_general(xs_ref[...], qt_ref[...], (((1,), (1,)), ((), ())),
                                    preferred_element_type=F32)
                  + dsk_ref[...] * u.astype(F32))


def _ssm(u2, wc, qc, kd, lam, dsk, B):
    R = u2.shape[1]
    nch = R // B
    rt = min(256, R)
    wide = CHUNK * LANES
    nstate = 2 * GROUPS_PER_BLOCK * SSM_STATE
    xs = pl.pallas_call(
        functools.partial(_ssm_state_kernel, B=B, nch=nch, rt=rt),
        grid=(SSM_BLOCKS,),
        in_specs=[
            pl.BlockSpec((None, R, wide), lambda j: (j, 0, 0)),
            pl.BlockSpec((None, wide, 2 * SSM_STATE), lambda j: (j, 0, 0)),
            pl.BlockSpec((None, 1, nstate), lambda j: (j, 0, 0)),
        ],
        out_specs=pl.BlockSpec((None, R, nstate), lambda j: (j, 0, 0)),
        out_shape=jax.ShapeDtypeStruct((SSM_BLOCKS, R, nstate), BF16),
        scratch_shapes=[pltpu.VMEM((wide, nstate), BF16), pltpu.VMEM((R, nstate), F32),
                        pltpu.VMEM((R, nstate), F32)],
        compiler_params=_params("parallel"),
        name="ssm_state",
    )(u2, wc, lam)
    return pl.pallas_call(
        _ssm_out_kernel,
        grid=(SSM_BLOCKS, R // rt),
        in_specs=[
            pl.BlockSpec((None, rt, wide), lambda j, r: (j, r, 0)),
            pl.BlockSpec((None, rt, nstate), lambda j, r: (j, r, 0)),
            pl.BlockSpec((None, LANES, CHUNK * SSM_GROUP), lambda j, r: (j, 0, 0)),
            pl.BlockSpec((None, wide, 2 * SSM_STATE), lambda j, r: (j, 0, 0)),
            pl.BlockSpec((None, 1, wide), lambda j, r: (j, 0, 0)),
        ],
        out_specs=pl.BlockSpec((None, rt, wide), lambda j, r: (j, r, 0)),
        out_shape=jax.ShapeDtypeStruct((SSM_BLOCKS, R, wide), F32),
        scratch_shapes=[pltpu.VMEM((wide, wide), BF16), pltpu.VMEM((wide, nstate), BF16)],
        compiler_params=_params("arbitrary", "arbitrary"),
        name="ssm_out",
    )(u2, xs, kd, qc, dsk)


def _odd_out_kernel(x_ref, y_ref, p_ref, gw_ref, gb_ref, ng_ref, sw_ref, sbt_ref, wo_ref,
                    g_ref, o_ref, yt_ref, mix_ref, *, tm):
    nssm = SSM_BLOCKS * LANES
    for j in range(SSM_BLOCKS):
        for s in range(CHUNK):
            yt_ref[j, pl.ds(s, tm // CHUNK, stride=CHUNK), :] = y_ref[j, :, s * LANES:(s + 1) * LANES]
    y = _gelu(jnp.concatenate([yt_ref[j] for j in range(SSM_BLOCKS)], axis=1))
    gate = jnp.dot(y.astype(BF16), gw_ref[...], preferred_element_type=F32) + gb_ref[...]
    mix_ref[:, :nssm] = (y * _sigmoid(gate)).astype(BF16)

    u0, v0 = 0, SGU_HEADS
    ssq = jnp.zeros((tm, 1), F32)
    for h in range(SGU_HEADS):
        zv = _gelu(p_ref[v0 + h].astype(F32))
        ssq = ssq + jnp.sum(zv * zv, axis=-1, keepdims=True)
    inv = lax.rsqrt(ssq / float(SGU_HEADS * LANES) + EPS)
    row = lax.broadcasted_iota(jnp.int32, (LANES, LANES), 0)
    col = lax.broadcasted_iota(jnp.int32, (LANES, LANES), 1)
    nblk = tm // LANES
    for h in range(SGU_HEADS):
        cols = slice(h * LANES, (h + 1) * LANES)
        v = (_gelu(p_ref[v0 + h].astype(F32)) * inv * ng_ref[:, cols]).astype(BF16)
        vcat = jnp.concatenate([v[r * LANES:(r + 1) * LANES, :] for r in range(nblk)], axis=1)
        w = jnp.where(col <= row, sw_ref[h], jnp.zeros((), BF16))
        mixed = jnp.dot(w, vcat, preferred_element_type=F32) + sbt_ref[:, h:h + 1]
        zu = _gelu(p_ref[u0 + h].astype(F32))
        for r in range(nblk):
            rows = slice(r * LANES, (r + 1) * LANES)
            mix_ref[rows, nssm + h * LANES:nssm + (h + 1) * LANES] = (
                zu[rows, :] * mixed[:, rows]).astype(BF16)
    m = jnp.dot(mix_ref[...], wo_ref[...], preferred_element_type=F32)
    o_ref[...] = x_ref[...] + _rms(m, g_ref[...])


def _odd_out(x, y8, p, glu_w, glu_b, sgu_g, sgu_w, sgu_bt, w_out, g, *, tm=256):
    T, D = x.shape
    tm = min(tm, T)
    nmix = w_out.shape[0]
    nssm = SSM_BLOCKS * LANES
    full2 = lambda a: pl.BlockSpec(a.shape, lambda i: (0, 0))
    return pl.pallas_call(
        functools.partial(_odd_out_kernel, tm=tm),
        grid=(T // tm,),
        in_specs=[
            pl.BlockSpec((tm, D), lambda i: (i, 0)),
            pl.BlockSpec((SSM_BLOCKS, tm // CHUNK, CHUNK * LANES), lambda i: (0, i, 0)),
            pl.BlockSpec((p.shape[0], tm, LANES), lambda i: (0, i, 0)),
            full2(glu_w), full2(glu_b), full2(sgu_g),
            pl.BlockSpec(sgu_w.shape, lambda i: (0, 0, 0)),
            full2(sgu_bt), full2(w_out), full2(g),
        ],
        out_specs=pl.BlockSpec((tm, D), lambda i: (i, 0)),
        out_shape=jax.ShapeDtypeStruct((T, D), F32),
        scratch_shapes=[pltpu.VMEM((SSM_BLOCKS, tm, LANES), F32), pltpu.VMEM((tm, nmix), BF16)],
        compiler_params=_params("parallel"),
        name="odd_out",
    )(x, y8, p, glu_w, glu_b, sgu_g, sgu_w, sgu_bt, w_out, g)


def kernel(x, norm_g, ffn_w_gate, ffn_w_up, ffn_w_down, ev_w_in, ev_pool_w, ev_pool_scale, ev_w_out, od_w_in, od_ssm_a_re, od_ssm_a_im, od_ssm_log_dt, od_ssm_b_re, od_ssm_b_im, od_ssm_c_re, od_ssm_c_im, od_ssm_d, od_glu_w, od_glu_b, od_sgu_norm_g, od_sgu_w, od_sgu_b, od_w_out):
    B, L, D = x.shape
    depth = norm_g.shape[0]
    h = x.reshape(B * L, D)
    row = lambda a: a.reshape(1, -1)
    bf = lambda a: a.astype(BF16)
    wg, wu, wd = bf(ffn_w_gate), bf(ffn_w_up), bf(ffn_w_down)
    for i in range(depth):
        g = norm_g[i]
        h = _ffn(h, row(g[0]), wg, wu, wd, row(g[1]), i, 0)
        j = i // 2
        if i % 2 == 0:
            w_in = ev_w_in[j]
            q_cols = (jnp.arange(w_in.shape[1]) // LANES >= N_POOL_BLOCKS) & (
                jnp.arange(w_in.shape[1]) // LANES < N_POOL_BLOCKS + SB_HEADS)
            col_scale = jnp.where(q_cols, LANES ** -0.5, 1.0).astype(F32).reshape(1, -1)
            p = _norm_proj(h, row(g[2]), bf(w_in), col_scale)
            ysb = _stick_breaking(p, B, L)
            h = _even_out(h, p, ysb, bf(ev_pool_w[j]), row(ev_pool_scale[j]), bf(ev_w_out[j]), row(g[3]), L)
        else:
            w_in = od_w_in[j]
            u2, p = _norm_proj(h, row(g[2]), bf(w_in), jnp.ones((1, w_in.shape[1]), F32), chunked=True)
            wc, qc, kd, lam = _ssm_prep(od_ssm_a_re[j], od_ssm_a_im[j], od_ssm_log_dt[j], od_ssm_b_re[j],
                                        od_ssm_b_im[j], od_ssm_c_re[j], od_ssm_c_im[j])
            dsk = jnp.tile(od_ssm_d[j].reshape(SSM_BLOCKS, 1, LANES), (1, 1, CHUNK))
            y8 = _ssm(u2, wc, qc, kd, lam, dsk, B)
            h = _odd_out(h, y8, p, bf(od_glu_w[j]), row(od_glu_b[j]), row(od_sgu_norm_g[j]),
                         bf(od_sgu_w[j]), jnp.transpose(od_sgu_b[j]), bf(od_w_out[j]), row(g[3]))
        h = _ffn(h, row(g[4]), wg, wu, wd, row(g[5]), i, 1)
    return h.reshape(B, L, D)
```
